```python
import jax, jax.numpy as jnp
from jax import lax
import numpy as np

D_MODEL = 1024
BATCH = 4
SEQ = 8192
DEPTH = 1

N_MEM = 256
HEAD_DIM = 64
N_ATTN_HEADS = 8
N_MEM_HEADS = 4
POOL_WINDOWS = (2, 4, 8, 16)
N_POOL_GROUPS = 4
POOL_GROUP_DIM = 64
D_ATTN = N_ATTN_HEADS * HEAD_DIM
D_POOL = N_POOL_GROUPS * POOL_GROUP_DIM
D_XMEM = N_MEM_HEADS * HEAD_DIM
D_MIX = D_ATTN + D_POOL + D_XMEM
D_IN = 3 * D_ATTN + D_POOL + D_XMEM
DILATED = ((128, 1), (512, 4), (2048, 16))
BLK = 128
D_FF = 4 * D_MODEL
EPS = 1e-6

kernel_name = 'hymba_dilated_pool_memory_layer'


def rmsnorm(x, g):
    xf = x.astype(jnp.float32)
    xf = xf * lax.rsqrt(jnp.mean(xf * xf, axis=-1, keepdims=True) + EPS)
    return xf.astype(x.dtype) * g


def alibi_slopes(n_heads):
    return 2.0 ** (-8.0 * jnp.arange(1, n_heads + 1, dtype=jnp.float32) / n_heads)


def band_blocks(t, n_prev, nb):
    N, H, _, Dh = t.shape
    tp = jnp.pad(t, ((0, 0), (0, 0), (n_prev * BLK, 0), (0, 0)))
    tb = tp.reshape(N, H, nb + n_prev, BLK, Dh)
    return jnp.concatenate([tb[:, :, o:o + nb] for o in range(n_prev + 1)], axis=3)


def dilated_window_attention(q, k, v, window, dilation, slopes):
    B, S, H, Dh = q.shape
    L = S // dilation
    Lp = -(-L // BLK) * BLK
    nb = Lp // BLK
    span = window // dilation
    n_prev = -(-span // BLK)
    N = B * dilation

    def to_sub(t):
        t = t.reshape(B, L, dilation, H, Dh).transpose(0, 2, 3, 1, 4).reshape(N, H, L, Dh)
        return jnp.pad(t, ((0, 0), (0, 0), (0, Lp - L), (0, 0)))

    qb = to_sub(q).reshape(N, H, nb, BLK, Dh)
    kb = band_blocks(to_sub(k), n_prev, nb)
    vb = band_blocks(to_sub(v), n_prev, nb)
    KB = (n_prev + 1) * BLK
    s = jnp.einsum('nhbqd,nhbkd->nhbqk', qb, kb).astype(jnp.float32) * (Dh ** -0.5)
    qi = jnp.arange(BLK)[:, None]
    ki = jnp.arange(KB)[None, :]
    rel = n_prev * BLK + qi - ki
    key_idx = (jnp.arange(nb)[:, None, None] - n_prev) * BLK + ki[None]
    valid = (rel >= 0) & (rel <= span) & (key_idx >= 0)
    bias = -slopes[:, None, None, None] * (dilation * rel).astype(jnp.float32)
    s = jnp.where(valid, s + bias, -jnp.inf)
    m = jnp.max(s, axis=-1, keepdims=True)
    p = jnp.exp(s - m)
    den = jnp.sum(p, axis=-1, keepdims=True)
    o = jnp.einsum('nhbqk,nhbkd->nhbqd', (p / den).astype(v.dtype), vb)
    lse = (m + jnp.log(den))[..., 0]
    o = o.reshape(N, H, Lp, Dh)[:, :, :L].reshape(B, dilation, H, L, Dh)
    o = o.transpose(0, 3, 1, 2, 4).reshape(B, S, H, Dh)
    lse = lse.reshape(N, H, Lp)[:, :, :L].reshape(B, dilation, H, L)
    lse = lse.transpose(0, 3, 1, 2).reshape(B, S, H)
    return o, lse


def causal_multiscale_pool(u):
    B, S, _ = u.shape
    ug = u.astype(jnp.float32).reshape(B, S, N_POOL_GROUPS, POOL_GROUP_DIM)
    cs = jnp.pad(jnp.cumsum(ug, axis=1), ((0, 0), (1, 0), (0, 0), (0, 0)))
    t = jnp.arange(S)
    outs = []
    for g, w in enumerate(POOL_WINDOWS):
        lo = jnp.maximum(t + 1 - w, 0)
        total = cs[:, 1:, g] - cs[:, lo, g]
        cnt = (t + 1 - lo).astype(jnp.float32)
        outs.append(total / cnt[None, :, None])
    pooled = jnp.stack(outs, axis=2)
    return (pooled - ug).astype(u.dtype)


def setup_inputs(seed: int = 0) -> dict:
    key = jax.random.key(seed)
    ks = jax.random.split(key, 13)

    def nrm(k, shape, fan_in):
        return jax.random.normal(k, shape, jnp.float32) * (fan_in ** -0.5)

    def gain(k, shape):
        return 1.0 + 0.1 * jax.random.normal(k, shape, jnp.float32)

    return {
        'x': jax.random.normal(ks[0], (BATCH, SEQ, D_MODEL), jnp.float32),
        'mem': jax.random.normal(ks[1], (BATCH, N_MEM, D_MODEL), jnp.float32),
        'g_mix': gain(ks[2], (DEPTH, D_MODEL)),
        'w_in': nrm(ks[3], (DEPTH, D_MODEL, D_IN), D_MODEL),
        'g_mem': gain(ks[4], (DEPTH, D_MODEL)),
        'w_mem_kv': nrm(ks[5], (DEPTH, D_MODEL, 2 * D_XMEM), D_MODEL),
        'w_pool': nrm(ks[6], (DEPTH, N_POOL_GROUPS, POOL_GROUP_DIM, POOL_GROUP_DIM), POOL_GROUP_DIM),
        'pool_scale': gain(ks[7], (DEPTH, D_POOL)),
        'w_out': nrm(ks[8], (DEPTH, D_MIX, D_MODEL), D_MIX),
        'g_ffn': gain(ks[9], (DEPTH, D_MODEL)),
        'w_ff1': nrm(ks[10], (DEPTH, D_MODEL, D_FF), D_MODEL),
        'w_ff2': nrm(ks[11], (DEPTH, D_FF, D_MODEL), D_FF),
        'g_final': gain(ks[12], (D_MODEL,)),
    }


def reference(x, mem, g_mix, w_in, g_mem, w_mem_kv, w_pool, pool_scale, w_out,
              g_ffn, w_ff1, w_ff2, g_final):
    B, S, _ = x.shape
    slopes = alibi_slopes(N_ATTN_HEADS)
    for i in range(DEPTH):
        h = rmsnorm(x, g_mix[i])
        proj = h @ w_in[i]
        q, k, v, u, qm = jnp.split(
            proj, [D_ATTN, 2 * D_ATTN, 3 * D_ATTN, 3 * D_ATTN + D_POOL], axis=-1)
        q = q.reshape(B, S, N_ATTN_HEADS, HEAD_DIM)
        k = k.reshape(B, S, N_ATTN_HEADS, HEAD_DIM)
        v = v.reshape(B, S, N_ATTN_HEADS, HEAD_DIM)

        outs, lses = [], []
        for window, dilation in DILATED:
            o, l = dilated_window_attention(q, k, v, window, dilation, slopes)
            outs.append(o)
            lses.append(l)
        wts = jax.nn.softmax(jnp.stack(lses, axis=0), axis=0)
        y_attn = jnp.sum(wts[..., None] * jnp.stack(outs, axis=0).astype(jnp.float32), axis=0)
        y_attn = y_attn.astype(x.dtype).reshape(B, S, D_ATTN)

        d = causal_multiscale_pool(u)
        y_pool = jnp.einsum('bsgc,gce->bsge', d, w_pool[i]).reshape(B, S, D_POOL) * pool_scale[i]

        mn = rmsnorm(mem, g_mem[i])
        km, vm = jnp.split(mn @ w_mem_kv[i], 2, axis=-1)
        km = km.reshape(B, N_MEM, N_MEM_HEADS, HEAD_DIM)
        vm = vm.reshape(B, N_MEM, N_MEM_HEADS, HEAD_DIM)
        qm = qm.reshape(B, S, N_MEM_HEADS, HEAD_DIM)
        sm = jnp.einsum('bshd,bmhd->bhsm', qm, km).astype(jnp.float32) * (HEAD_DIM ** -0.5)
        pm = jax.nn.softmax(sm, axis=-1).astype(vm.dtype)
        y_mem = jnp.einsum('bhsm,bmhd->bshd', pm, vm).reshape(B, S, D_XMEM)

        y = jnp.concatenate([y_attn, y_pool, y_mem], axis=-1)
        x = x + y @ w_out[i]

        h2 = rmsnorm(x, g_ffn[i])
        a = jax.nn.relu(h2 @ w_ff1[i])
        x = x + (a * a) @ w_ff2[i]
    return rmsnorm(x, g_final)
```

```python
import functools

import numpy as np
import jax
import jax.numpy as jnp
from jax import lax
from jax.experimental import pallas as pl
from jax.experimental.pallas import tpu as pltpu

D_MODEL = 1024
HEAD_DIM = 64
N_ATTN_HEADS = 8
N_MEM_HEADS = 4
POOL_WINDOWS = (2, 4, 8, 16)
POOL_GROUP_DIM = 64
D_ATTN = N_ATTN_HEADS * HEAD_DIM
D_POOL = len(POOL_WINDOWS) * POOL_GROUP_DIM
D_XMEM = N_MEM_HEADS * HEAD_DIM
DILATED = ((128, 1), (512, 4), (2048, 16))
BLK = 128
SPAN = 128
D_FF = 4 * D_MODEL
EPS = 1e-6
POOL_HALO = 16

BF16 = jnp.bfloat16
F32 = jnp.float32

VMEM_LIMIT_BYTES = 56 * 1024 * 1024

assert all(w // d == SPAN for w, d in DILATED)
assert max(POOL_WINDOWS) <= POOL_HALO


def _rmsnorm(x, g):
    ms = jnp.mean(x * x, axis=-1, keepdims=True)
    return (x * lax.rsqrt(ms + EPS)) * g


def _resident(shape):
    return pl.BlockSpec(shape, lambda *_: (0,) * len(shape), pipeline_mode=pl.Buffered(1))


def _memkv_kernel(mem_ref, g_ref, w_ref, km_ref, vm_ref):
    mn = _rmsnorm(mem_ref[...], g_ref[...]).astype(BF16)
    kv = jnp.dot(mn, w_ref[...], preferred_element_type=F32)
    km_ref[...] = kv[:, :D_XMEM].astype(BF16)
    vm_ref[...] = kv[:, D_XMEM:].astype(BF16)


def _memkv(mem, g_mem, w_mem_kv):
    B, M, _ = mem.shape
    return pl.pallas_call(
        _memkv_kernel,
        grid=(B,),
        in_specs=[
            pl.BlockSpec((None, M, D_MODEL), lambda b: (b, 0, 0)),
            _resident((1, D_MODEL)),
            _resident((D_MODEL, 2 * D_XMEM)),
        ],
        out_specs=[
            pl.BlockSpec((None, M, D_XMEM), lambda b: (b, 0, 0)),
            pl.BlockSpec((None, M, D_XMEM), lambda b: (b, 0, 0)),
        ],
        out_shape=[jax.ShapeDtypeStruct((B, M, D_XMEM), BF16)] * 2,
    )(mem, g_mem, w_mem_kv)


def _inproj_kernel(x_ref, g_ref, w_ref, wpool_ref, pscale_ref,
                   q_ref, k_ref, v_ref, qm_ref, yp_ref, ext_ref, *, tm):
    t = pl.program_id(1)
    h = _rmsnorm(x_ref[...], g_ref[...]).astype(BF16)

    def proj(lo, hi):
        return jnp.dot(h, w_ref[:, lo:hi], preferred_element_type=F32)

    scale = HEAD_DIM ** -0.5
    q_ref[...] = (proj(0, D_ATTN) * scale).astype(BF16)
    k_ref[...] = proj(D_ATTN, 2 * D_ATTN).astype(BF16)
    v_ref[...] = proj(2 * D_ATTN, 3 * D_ATTN).astype(BF16)
    qm_ref[...] = (proj(3 * D_ATTN + D_POOL, 3 * D_ATTN + D_POOL + D_XMEM) * scale).astype(BF16)

    u = proj(3 * D_ATTN, 3 * D_ATTN + D_POOL)

    @pl.when(t == 0)
    def _():
        ext_ref[0:POOL_HALO, :] = jnp.zeros((POOL_HALO, D_POOL), F32)

    ext_ref[POOL_HALO:POOL_HALO + tm, :] = u
    group = lax.broadcasted_iota(jnp.int32, (1, D_POOL), 1) // POOL_GROUP_DIM
    win = jnp.zeros((1, D_POOL), jnp.int32)
    for g, w in enumerate(POOL_WINDOWS):
        win = jnp.where(group == g, w, win)
    total = u
    for j in range(1, max(POOL_WINDOWS)):
        total = total + jnp.where(win > j, ext_ref[pl.ds(POOL_HALO - j, tm), :], 0.0)
    pos = t * tm + lax.broadcasted_iota(jnp.int32, (tm, 1), 0)
    cnt = jnp.minimum(win, pos + 1).astype(F32)
    d = (total / cnt - u).astype(BF16)
    yp = jnp.dot(d, wpool_ref[...], preferred_element_type=F32) * pscale_ref[...]
    yp_ref[...] = yp.astype(BF16)
    ext_ref[0:POOL_HALO, :] = ext_ref[tm:tm + POOL_HALO, :]


def _inproj(x, g_mix, w_in, wpool_bd, pool_scale, *, tm=512):
    B, S, _ = x.shape
    d_in = w_in.shape[1]
    row = lambda b, t: (b, t, 0)
    return pl.pallas_call(
        functools.partial(_inproj_kernel, tm=tm),
        grid=(B, S // tm),
        in_specs=[
            pl.BlockSpec((None, tm, D_MODEL), row),
            _resident((1, D_MODEL)),
            _resident((D_MODEL, d_in)),
            _resident((D_POOL, D_POOL)),
            _resident((1, D_POOL)),
        ],
        out_specs=[
            pl.BlockSpec((None, tm, D_ATTN), row),
            pl.BlockSpec((None, tm, D_ATTN), row),
            pl.BlockSpec((None, tm, D_ATTN), row),
            pl.BlockSpec((None, tm, D_XMEM), row),
            pl.BlockSpec((None, tm, D_POOL), row),
        ],
        out_shape=[
            jax.ShapeDtypeStruct((B, S, D_ATTN), BF16),
            jax.ShapeDtypeStruct((B, S, D_ATTN), BF16),
            jax.ShapeDtypeStruct((B, S, D_ATTN), BF16),
            jax.ShapeDtypeStruct((B, S, D_XMEM), BF16),
            jax.ShapeDtypeStruct((B, S, D_POOL), BF16),
        ],
        scratch_shapes=[pltpu.VMEM((tm + POOL_HALO, D_POOL), F32)],
        compiler_params=pltpu.CompilerParams(
            dimension_semantics=("arbitrary", "arbitrary"),
            vmem_limit_bytes=VMEM_LIMIT_BYTES),
    )(x, g_mix, w_in, wpool_bd, pool_scale)


def _alibi_bias_table(dilation):
    slopes = (2.0 ** (-8.0 * np.arange(1, N_ATTN_HEADS + 1, dtype=np.float32)
                      / N_ATTN_HEADS)).astype(np.float32)
    qi = np.arange(BLK)[:, None]
    ki = np.arange(2 * BLK)[None, :]
    rel = BLK + qi - ki
    valid = (rel >= 0) & (rel <= SPAN)
    bias = -slopes[:, None, None] * (dilation * rel).astype(np.float32)[None]
    full = np.where(valid[None], bias, -np.inf)
    first = np.where((valid & (ki >= BLK))[None], bias, -np.inf)
    return np.stack([first, full]).astype(np.float32)


def _head_pair_attention(qp, kp, vp, bias_of_head):
    lane = lax.broadcasted_iota(jnp.int32, (1, 2 * HEAD_DIM), 1)
    is_lo = lane < HEAD_DIM
    outs, lses = [], []
    for hh in range(2):
        qh = jnp.where(is_lo if hh == 0 else jnp.logical_not(is_lo), qp, jnp.zeros_like(qp))
        s = lax.dot_general(qh, kp, (((1,), (1,)), ((), ())), preferred_element_type=F32)
        bias = bias_of_head(hh)
        if bias is not None:
            s = s + bias
        m = jnp.max(s, axis=-1, keepdims=True)
        p = jnp.exp(s - m)
        l = jnp.sum(p, axis=-1, keepdims=True)
        pv = jnp.dot(p.astype(BF16), vp, preferred_element_type=F32)
        outs.append(pv / l)
        lses.append(m + jnp.log(l))
    return jnp.where(is_lo, outs[0], outs[1]), lses


def _band_attn_kernel(q_ref, kh_ref, kc_ref, vh_ref, vc_ref, bias_ref,
                      o_ref, lse_ref, kf_ref, vf_ref, *, tq):
    i = pl.program_id(1)
    kf_ref[0:BLK, :] = kh_ref[...]
    kf_ref[BLK:, :] = kc_ref[...]
    vf_ref[0:BLK, :] = vh_ref[...]
    vf_ref[BLK:, :] = vc_ref[...]
    head_lane = lax.broadcasted_iota(jnp.int32, (1, N_ATTN_HEADS), 1)

    def body(j, carry):
        r0 = pl.multiple_of(j * BLK, BLK)
        sel = jnp.where(jnp.logical_and(i == 0, j == 0), 0, 1)
        lse_tile = jnp.zeros((BLK, N_ATTN_HEADS), F32)
        for hp in range(N_ATTN_HEADS // 2):
            cols = slice(hp * 2 * HEAD_DIM, (hp + 1) * 2 * HEAD_DIM)
            o_pair, lses = _head_pair_attention(
                q_ref[pl.ds(r0, BLK), cols],
                kf_ref[pl.ds(r0, 2 * BLK), cols],
                vf_ref[pl.ds(r0, 2 * BLK), cols],
                lambda hh: bias_ref[sel, 2 * hp + hh])
            o_ref[pl.ds(r0, BLK), cols] = o_pair.astype(BF16)
            for hh in range(2):
                lse_tile = jnp.where(head_lane == 2 * hp + hh, lses[hh], lse_tile)
        lse_ref[pl.ds(r0, BLK), :] = lse_tile
        return carry

    lax.fori_loop(0, tq // BLK, body, 0)


def _band_attn(q, k, v, dilation):
    N, L, _ = q.shape
    tq = min(L, 1024)
    sub = tq // BLK
    bias = jnp.asarray(_alibi_bias_table(dilation))
    cur = lambda n, i: (n, i, 0)
    halo = lambda n, i: (n, jnp.maximum(i * sub - 1, 0), 0)
    return pl.pallas_call(
        functools.partial(_band_attn_kernel, tq=tq),
        grid=(N, L // tq),
        in_specs=[
            pl.BlockSpec((None, tq, D_ATTN), cur),
            pl.BlockSpec((None, BLK, D_ATTN), halo),
            pl.BlockSpec((None, tq, D_ATTN), cur),
            pl.BlockSpec((None, BLK, D_ATTN), halo),
            pl.BlockSpec((None, tq, D_ATTN), cur),
            _resident(bias.shape),
        ],
        out_specs=[
            pl.BlockSpec((None, tq, D_ATTN), cur),
            pl.BlockSpec((None, tq, N_ATTN_HEADS), cur),
        ],
        out_shape=[
            jax.ShapeDtypeStruct((N, L, D_ATTN), BF16),
            jax.ShapeDtypeStruct((N, L, N_ATTN_HEADS), F32),
        ],
        scratch_shapes=[pltpu.VMEM((tq + BLK, D_ATTN), BF16)] * 2,
        compiler_params=pltpu.CompilerParams(
            dimension_semantics=("parallel", "parallel"),
            vmem_limit_bytes=VMEM_LIMIT_BYTES),
    )(q, k, k, v, v, bias)


def _mem_attn_kernel(qm_ref, km_ref, vm_ref, y_ref, *, tm, rows):
    def body(j, carry):
        r0 = pl.multiple_of(j * rows, rows)
        for hp in range(N_MEM_HEADS // 2):
            cols = slice(hp * 2 * HEAD_DIM, (hp + 1) * 2 * HEAD_DIM)
            o_pair, _ = _head_pair_attention(
                qm_ref[pl.ds(r0, rows), cols], km_ref[:, cols], vm_ref[:, cols],
                lambda hh: None)
            y_ref[pl.ds(r0, rows), cols] = o_pair.astype(BF16)
        return carry

    lax.fori_loop(0, tm // rows, body, 0)


def _mem_attn(qm, km, vm, *, tm=1024, rows=256):
    B, S, _ = qm.shape
    M = km.shape[1]
    return pl.pallas_call(
        functools.partial(_mem_attn_kernel, tm=tm, rows=rows),
        grid=(B, S // tm),
        in_specs=[
            pl.BlockSpec((None, tm, D_XMEM), lambda b, t: (b, t, 0)),
            pl.BlockSpec((None, M, D_XMEM), lambda b, t: (b, 0, 0)),
            pl.BlockSpec((None, M, D_XMEM), lambda b, t: (b, 0, 0)),
        ],
        out_specs=pl.BlockSpec((None, tm, D_XMEM), lambda b, t: (b, t, 0)),
        out_shape=jax.ShapeDtypeStruct((B, S, D_XMEM), BF16),
        compiler_params=pltpu.CompilerParams(
            dimension_semantics=("parallel", "parallel"),
            vmem_limit_bytes=VMEM_LIMIT_BYTES),
    )(qm, km, vm)


def _out_ffn_kernel(x_ref, o1_ref, o4_ref, o16_ref, l1_ref, l4_ref, l16_ref,
                    yp_ref, ym_ref, expand_ref, wout_ref, gffn_ref, w1_ref, w2_ref,
                    gfin_ref, out_ref, *, ff_chunk):
    lses = [l1_ref[...], l4_ref[...], l16_ref[...]]
    top = jnp.maximum(jnp.maximum(lses[0], lses[1]), lses[2])
    es = [jnp.exp(l - top) for l in lses]
    den = es[0] + es[1] + es[2]

    def expand(w):
        hi = w.astype(BF16)
        lo = (w - hi.astype(F32)).astype(BF16)
        e = expand_ref[...]
        return (jnp.dot(hi, e, preferred_element_type=F32)
                + jnp.dot(lo, e, preferred_element_type=F32))

    y_attn = jnp.zeros(o1_ref.shape, F32)
    for e, o_ref in zip(es, (o1_ref, o4_ref, o16_ref)):
        y_attn = y_attn + expand(e / den) * o_ref[...].astype(F32)

    mix = jnp.dot(y_attn.astype(BF16), wout_ref[0:D_ATTN, :], preferred_element_type=F32)
    mix = mix + jnp.dot(yp_ref[...], wout_ref[D_ATTN:D_ATTN + D_POOL, :],
                        preferred_element_type=F32)
    mix = mix + jnp.dot(ym_ref[...], wout_ref[D_ATTN + D_POOL:, :],
                        preferred_element_type=F32)
    x1 = x_ref[...] + mix

    h2 = _rmsnorm(x1, gffn_ref[...]).astype(BF16)
    ff = jnp.zeros(x1.shape, F32)
    for c in range(D_FF // ff_chunk):
        a = jnp.dot(h2, w1_ref[:, c * ff_chunk:(c + 1) * ff_chunk], preferred_element_type=F32)
        a = jnp.maximum(a, 0.0)
        ff = ff + jnp.dot((a * a).astype(BF16), w2_ref[c * ff_chunk:(c + 1) * ff_chunk, :],
                          preferred_element_type=F32)
    out_ref[...] = _rmsnorm(x1 + ff, gfin_ref[...])


def _out_ffn(x, o_list, lse_list, y_pool, y_mem, w_out, g_ffn, w_ff1, w_ff2, g_final,
             *, tm=512, ff_chunk=1024):
    B, S, _ = x.shape
    expand = jnp.asarray(np.repeat(np.eye(N_ATTN_HEADS, dtype=np.float32), HEAD_DIM, axis=1),
                         dtype=BF16)
    row = lambda b, t: (b, t, 0)
    tile = lambda width: pl.BlockSpec((None, tm, width), row)
    return pl.pallas_call(
        functools.partial(_out_ffn_kernel, ff_chunk=ff_chunk),
        grid=(B, S // tm),
        in_specs=[
            tile(D_MODEL),
            tile(D_ATTN), tile(D_ATTN), tile(D_ATTN),
            tile(N_ATTN_HEADS), tile(N_ATTN_HEADS), tile(N_ATTN_HEADS),
            tile(D_POOL), tile(D_XMEM),
            _resident(expand.shape),
            _resident(w_out.shape),
            _resident((1, D_MODEL)),
            _resident(w_ff1.shape),
            _resident(w_ff2.shape),
            _resident((1, D_MODEL)),
        ],
        out_specs=tile(D_MODEL),
        out_shape=jax.ShapeDtypeStruct((B, S, D_MODEL), F32),
        compiler_params=pltpu.CompilerParams(
            dimension_semantics=("parallel", "parallel"),
            vmem_limit_bytes=VMEM_LIMIT_BYTES),
    )(x, *o_list, *lse_list, y_pool, y_mem, expand, w_out, g_ffn, w_ff1, w_ff2, g_final)


def _to_residues(t, dilation):
    if dilation == 1:
        return t
    B, S, C = t.shape
    t = t.reshape(B, S // dilation, dilation, C).transpose(0, 2, 1, 3)
    return t.reshape(B * dilation, S // dilation, C)


def _from_residues(t, dilation, B):
    if dilation == 1:
        return t
    N, L, C = t.shape
    t = t.reshape(B, dilation, L, C).transpose(0, 2, 1, 3)
    return t.reshape(B, L * dilation, C)


def _block_diag(w_pool):
    G, C, E = w_pool.shape
    out = jnp.zeros((G * C, G * E), w_pool.dtype)
    for g in range(G):
        out = out.at[g * C:(g + 1) * C, g * E:(g + 1) * E].set(w_pool[g])
    return out


def kernel(x, mem, g_mix, w_in, g_mem, w_mem_kv, w_pool, pool_scale, w_out,
           g_ffn, w_ff1, w_ff2, g_final):
    B, S, _ = x.shape
    depth = w_in.shape[0]
    for i in range(depth):
        km, vm = _memkv(mem, g_mem[i][None], w_mem_kv[i].astype(BF16))
        q, k, v, qm, y_pool = _inproj(
            x, g_mix[i][None], w_in[i].astype(BF16),
            _block_diag(w_pool[i]).astype(BF16), pool_scale[i][None])

        o_list, lse_list = [], []
        for _, dilation in DILATED:
            o, lse = _band_attn(*(_to_residues(t, dilation) for t in (q, k, v)), dilation)
            o_list.append(_from_residues(o, dilation, B))
            lse_list.append(_from_residues(lse, dilation, B))
        y_mem = _mem_attn(qm, km, vm)

        assert depth == 1
        x = _out_ffn(x, o_list, lse_list, y_pool, y_mem, w_out[i].astype(BF16),
                     g_ffn[i][None], w_ff1[i].astype(BF16), w_ff2[i].astype(BF16),
                     g_final[None])
    return x
```

```python
import functools

import numpy as np
import jax
import jax.numpy as jnp
from jax import lax
from jax.experimental import pallas as pl
from jax.experimental.pallas import tpu as pltpu

D_MODEL = 1024
HEAD_DIM = 64
N_ATTN_HEADS = 8
N_MEM_HEADS = 4
POOL_WINDOWS = (2, 4, 8, 16)
POOL_GROUP_DIM = 64
D_ATTN = N_ATTN_HEADS * HEAD_DIM
D_POOL = len(POOL_WINDOWS) * POOL_GROUP_DIM
D_XMEM = N_MEM_HEADS * HEAD_DIM
DILATED = ((128, 1), (512, 4), (2048, 16))
BLK = 128
SPAN = 128
D_FF = 4 * D_MODEL
EPS = 1e-6
POOL_HALO = 16

BF16 = jnp.bfloat16
F32 = jnp.float32
LOG2E = 1.4426950408889634
LN2 = 0.6931471805599453

VMEM_LIMIT_BYTES = 56 * 1024 * 1024

assert all(w // d == SPAN for w, d in DILATED)
assert max(POOL_WINDOWS) <= POOL_HALO


def _rmsnorm(x, g):
    ms = jnp.mean(x * x, axis=-1, keepdims=True)
    return (x * lax.rsqrt(ms + EPS)) * g


def _resident(shape):
    return pl.BlockSpec(shape, lambda *_: (0,) * len(shape), pipeline_mode=pl.Buffered(1))


def _memkv_kernel(mem_ref, g_ref, w_ref, km_ref, vm_ref):
    mn = _rmsnorm(mem_ref[...], g_ref[...]).astype(BF16)
    kv = jnp.dot(mn, w_ref[...], preferred_element_type=F32)
    km_ref[...] = kv[:, :D_XMEM].astype(BF16)
    vm_ref[...] = kv[:, D_XMEM:].astype(BF16)


def _memkv(mem, g_mem, w_mem_kv):
    B, M, _ = mem.shape
    return pl.pallas_call(
        _memkv_kernel,
        grid=(B,),
        in_specs=[
            pl.BlockSpec((None, M, D_MODEL), lambda b: (b, 0, 0)),
            _resident((1, D_MODEL)),
            _resident((D_MODEL, 2 * D_XMEM)),
        ],
        out_specs=[
            pl.BlockSpec((None, M, D_XMEM), lambda b: (b, 0, 0)),
            pl.BlockSpec((None, M, D_XMEM), lambda b: (b, 0, 0)),
        ],
        out_shape=[jax.ShapeDtypeStruct((B, M, D_XMEM), BF16)] * 2,
    )(mem, g_mem, w_mem_kv)


def _inproj_kernel(x_ref, g_ref, w_ref, wpool_ref, pscale_ref,
                   q_ref, k_ref, v_ref, qm_ref, yp_ref, ext_ref, *, tm):
    t = pl.program_id(1)
    h = _rmsnorm(x_ref[...], g_ref[...]).astype(BF16)

    def proj(lo, hi):
        return jnp.dot(h, w_ref[:, lo:hi], preferred_element_type=F32)

    scale = LOG2E * HEAD_DIM ** -0.5
    q_ref[...] = (proj(0, D_ATTN) * scale).astype(BF16)
    k_ref[...] = proj(D_ATTN, 2 * D_ATTN).astype(BF16)
    v_ref[...] = proj(2 * D_ATTN, 3 * D_ATTN).astype(BF16)
    qm_ref[...] = (proj(3 * D_ATTN + D_POOL, 3 * D_ATTN + D_POOL + D_XMEM) * scale).astype(BF16)

    u = proj(3 * D_ATTN, 3 * D_ATTN + D_POOL)

    @pl.when(t == 0)
    def _():
        ext_ref[0:POOL_HALO, :] = jnp.zeros((POOL_HALO, D_POOL), F32)

    ext_ref[POOL_HALO:POOL_HALO + tm, :] = u
    group = lax.broadcasted_iota(jnp.int32, (1, D_POOL), 1) // POOL_GROUP_DIM
    win = jnp.zeros((1, D_POOL), jnp.int32)
    for g, w in enumerate(POOL_WINDOWS):
        win = jnp.where(group == g, w, win)
    total = u
    for j in range(1, max(POOL_WINDOWS)):
        total = total + jnp.where(win > j, ext_ref[pl.ds(POOL_HALO - j, tm), :], 0.0)
    pos = t * tm + lax.broadcasted_iota(jnp.int32, (tm, 1), 0)
    cnt = jnp.minimum(win, pos + 1).astype(F32)
    d = (total / cnt - u).astype(BF16)
    yp = jnp.dot(d, wpool_ref[...], preferred_element_type=F32) * pscale_ref[...]
    yp_ref[...] = yp.astype(BF16)
    ext_ref[0:POOL_HALO, :] = ext_ref[tm:tm + POOL_HALO, :]


def _inproj(x, g_mix, w_in, wpool_bd, pool_scale, *, tm=512):
    B, S, _ = x.shape
    d_in = w_in.shape[1]
    row = lambda b, t: (b, t, 0)
    return pl.pallas_call(
        functools.partial(_inproj_kernel, tm=tm),
        grid=(B, S // tm),
        in_specs=[
            pl.BlockSpec((None, tm, D_MODEL), row),
            _resident((1, D_MODEL)),
            _resident((D_MODEL, d_in)),
            _resident((D_POOL, D_POOL)),
            _resident((1, D_POOL)),
        ],
        out_specs=[
            pl.BlockSpec((None, tm, D_ATTN), row),
            pl.BlockSpec((None, tm, D_ATTN), row),
            pl.BlockSpec((None, tm, D_ATTN), row),
            pl.BlockSpec((None, tm, D_XMEM), row),
            pl.BlockSpec((None, tm, D_POOL), row),
        ],
        out_shape=[
            jax.ShapeDtypeStruct((B, S, D_ATTN), BF16),
            jax.ShapeDtypeStruct((B, S, D_ATTN), BF16),
            jax.ShapeDtypeStruct((B, S, D_ATTN), BF16),
            jax.ShapeDtypeStruct((B, S, D_XMEM), BF16),
            jax.ShapeDtypeStruct((B, S, D_POOL), BF16),
        ],
        scratch_shapes=[pltpu.VMEM((tm + POOL_HALO, D_POOL), F32)],
        compiler_params=pltpu.CompilerParams(
            dimension_semantics=("arbitrary", "arbitrary"),
            vmem_limit_bytes=VMEM_LIMIT_BYTES),
    )(x, g_mix, w_in, wpool_bd, pool_scale)


def _alibi_bias_table(dilation):
    slopes = (2.0 ** (-8.0 * np.arange(1, N_ATTN_HEADS + 1, dtype=np.float32)
                      / N_ATTN_HEADS)).astype(np.float32)
    qi = np.arange(BLK)[:, None]
    ki = np.arange(2 * BLK)[None, :]
    rel = BLK + qi - ki
    valid = (rel >= 0) & (rel <= SPAN)
    bias = -slopes[:, None, None] * (dilation * rel).astype(np.float32)[None] * np.float32(LOG2E)
    full = np.where(valid[None], bias, -np.inf)
    first = np.where((valid & (ki >= BLK))[None], bias, -np.inf)
    table = np.stack([first, full]).astype(np.float32)
    return table.reshape(2, N_ATTN_HEADS // 2, 2 * BLK, 2 * BLK)


def _pair_scores(qp, kp):
    is_lo = lax.broadcasted_iota(jnp.int32, (1, 2 * HEAD_DIM), 1) < HEAD_DIM
    zero = jnp.zeros_like(qp)
    q2 = jnp.concatenate([jnp.where(is_lo, qp, zero), jnp.where(is_lo, zero, qp)], axis=0)
    return lax.dot_general(q2, kp, (((1,), (1,)), ((), ())), preferred_element_type=F32)


def _pair_pv(p2, vp):
    vaug = jnp.concatenate([vp, jnp.ones_like(vp)], axis=1)
    return jnp.dot(p2.astype(BF16), vaug, preferred_element_type=F32)


def _pair_select(out2):
    R = out2.shape[0] // 2
    w = 2 * HEAD_DIM
    is_lo = lax.broadcasted_iota(jnp.int32, (1, w), 1) < HEAD_DIM
    num = jnp.where(is_lo, out2[:R, :w], out2[R:, :w])
    den = jnp.where(is_lo, out2[:R, w:], out2[R:, w:])
    return num, den


def _head_pair_attention(qp, kp, vp):
    s2 = _pair_scores(qp, kp)
    p2 = jnp.exp2(s2 - jnp.max(s2, axis=-1, keepdims=True))
    num, den = _pair_select(_pair_pv(p2, vp))
    return num / den


def _band_attn_kernel(q_ref, kh_ref, kc_ref, vh_ref, vc_ref, bias_ref,
                      o_ref, lse_ref, kf_ref, vf_ref, sa_ref, sb_ref, ma_ref, mb_ref, *, tq):
    i = pl.program_id(1)
    kf_ref[0:BLK, :] = kh_ref[...]
    kf_ref[BLK:, :] = kc_ref[...]
    vf_ref[0:BLK, :] = vh_ref[...]
    vf_ref[BLK:, :] = vc_ref[...]
    head_lane = lax.broadcasted_iota(jnp.int32, (1, N_ATTN_HEADS), 1)
    nsub = tq // BLK
    n_pairs = N_ATTN_HEADS // 2

    def scores(j, bufs):
        s_ref, m_ref = bufs
        r0 = pl.multiple_of(j * BLK, BLK)
        sel = jnp.where(jnp.logical_and(i == 0, j == 0), 0, 1)
        for hp in range(n_pairs):
            cols = slice(hp * 2 * HEAD_DIM, (hp + 1) * 2 * HEAD_DIM)
            s2 = _pair_scores(q_ref[pl.ds(r0, BLK), cols], kf_ref[pl.ds(r0, 2 * BLK), cols])
            s2 = s2 + bias_ref[sel, hp]
            m2 = jnp.max(s2, axis=-1, keepdims=True)
            s_ref[hp] = s2 - m2
            m_ref[hp] = jnp.broadcast_to(m2, (2 * BLK, BLK))

    def finish(j, bufs):
        s_ref, m_ref = bufs
        r0 = pl.multiple_of(j * BLK, BLK)
        row_max = jnp.zeros((BLK, N_ATTN_HEADS), F32)
        row_sum = jnp.ones((BLK, N_ATTN_HEADS), F32)
        for hp in range(n_pairs):
            cols = slice(hp * 2 * HEAD_DIM, (hp + 1) * 2 * HEAD_DIM)
            out2 = _pair_pv(jnp.exp2(s_ref[hp]), vf_ref[pl.ds(r0, 2 * BLK), cols])
            num, den = _pair_select(out2)
            o_ref[pl.ds(r0, BLK), cols] = (num / den).astype(BF16)
            for hh in range(2):
                rows = slice(hh * BLK, (hh + 1) * BLK)
                hit = head_lane == 2 * hp + hh
                row_max = jnp.where(hit, m_ref[hp, rows, 0:N_ATTN_HEADS], row_max)
                row_sum = jnp.where(
                    hit, out2[rows, 2 * HEAD_DIM:2 * HEAD_DIM + N_ATTN_HEADS], row_sum)
        lse_ref[pl.ds(r0, BLK), :] = LN2 * (row_max + jnp.log2(row_sum))

    assert nsub % 2 == 0
    buf_a, buf_b = (sa_ref, ma_ref), (sb_ref, mb_ref)
    scores(0, buf_a)

    def body(jj, carry):
        j = 2 * jj
        scores(j + 1, buf_b)
        finish(j, buf_a)
        scores(j + 2, buf_a)
        finish(j + 1, buf_b)
        return carry

    lax.fori_loop(0, nsub // 2 - 1, body, 0)
    scores(nsub - 1, buf_b)
    finish(nsub - 2, buf_a)
    finish(nsub - 1, buf_b)


def _band_attn(q, k, v, dilation):
    N, L, _ = q.shape
    tq = min(L, 1024)
    sub = tq // BLK
    bias = jnp.asarray(_alibi_bias_table(dilation))
    cur = lambda n, i: (n, i, 0)
    halo = lambda n, i: (n, jnp.maximum(i * sub - 1, 0), 0)
    return pl.pallas_call(
        functools.partial(_band_attn_kernel, tq=tq),
        grid=(N, L // tq),
        in_specs=[
            pl.BlockSpec((None, tq, D_ATTN), cur),
            pl.BlockSpec((None, BLK, D_ATTN), halo),
            pl.BlockSpec((None, tq, D_ATTN), cur),
            pl.BlockSpec((None, BLK, D_ATTN), halo),
            pl.BlockSpec((None, tq, D_ATTN), cur),
            _resident(bias.shape),
        ],
        out_specs=[
            pl.BlockSpec((None, tq, D_ATTN), cur),
            pl.BlockSpec((None, tq, N_ATTN_HEADS), cur),
        ],
        out_shape=[
            jax.ShapeDtypeStruct((N, L, D_ATTN), BF16),
            jax.ShapeDtypeStruct((N, L, N_ATTN_HEADS), F32),
        ],
        scratch_shapes=[pltpu.VMEM((tq + BLK, D_ATTN), BF16)] * 2
        + [pltpu.VMEM((N_ATTN_HEADS // 2, 2 * BLK, 2 * BLK), F32)] * 2
        + [pltpu.VMEM((N_ATTN_HEADS // 2, 2 * BLK, BLK), F32)] * 2,
        compiler_params=pltpu.CompilerParams(
            dimension_semantics=("parallel", "parallel"),
            vmem_limit_bytes=VMEM_LIMIT_BYTES),
    )(q, k, k, v, v, bias)


def _mem_attn_kernel(qm_ref, km_ref, vm_ref, y_ref, *, tm, rows):
    def body(j, carry):
        r0 = pl.multiple_of(j * rows, rows)
        for hp in range(N_MEM_HEADS // 2):
            cols = slice(hp * 2 * HEAD_DIM, (hp + 1) * 2 * HEAD_DIM)
            o_pair = _head_pair_attention(
                qm_ref[pl.ds(r0, rows), cols], km_ref[:, cols], vm_ref[:, cols])
            y_ref[pl.ds(r0, rows), cols] = o_pair.astype(BF16)
        return carry

    lax.fori_loop(0, tm // rows, body, 0)


def _mem_attn(qm, km, vm, *, tm=1024, rows=128):
    B, S, _ = qm.shape
    M = km.shape[1]
    return pl.pallas_call(
        functools.partial(_mem_attn_kernel, tm=tm, rows=rows),
        grid=(B, S // tm),
        in_specs=[
            pl.BlockSpec((None, tm, D_XMEM), lambda b, t: (b, t, 0)),
            pl.BlockSpec((None, M, D_XMEM), lambda b, t: (b, 0, 0)),
            pl.BlockSpec((None, M, D_XMEM), lambda b, t: (b, 0, 0)),
        ],
        out_specs=pl.BlockSpec((None, tm, D_XMEM), lambda b, t: (b, t, 0)),
        out_shape=jax.ShapeDtypeStruct((B, S, D_XMEM), BF16),
        compiler_params=pltpu.CompilerParams(
            dimension_semantics=("parallel", "parallel"),
            vmem_limit_bytes=VMEM_LIMIT_BYTES),
    )(qm, km, vm)


def _out_ffn_kernel(x_ref, o1_ref, o4_ref, o16_ref, l1_ref, l4_ref, l16_ref,
                    yp_ref, ym_ref, expand_ref, wout_ref, gffn_ref, w1_ref, w2_ref,
                    gfin_ref, out_ref, *, ff_chunk):
    lses = [l1_ref[...], l4_ref[...], l16_ref[...]]
    top = jnp.maximum(jnp.maximum(lses[0], lses[1]), lses[2])
    es = [jnp.exp(l - top) for l in lses]
    den = es[0] + es[1] + es[2]

    def expand(w):
        hi = w.astype(BF16)
        lo = (w - hi.astype(F32)).astype(BF16)
        e = expand_ref[...]
        return (jnp.dot(hi, e, preferred_element_type=F32)
                + jnp.dot(lo, e, preferred_element_type=F32))

    y_attn = jnp.zeros(o1_ref.shape, F32)
    for e, o_ref in zip(es, (o1_ref, o4_ref, o16_ref)):
        y_attn = y_attn + expand(e / den) * o_ref[...].astype(F32)

    mix = jnp.dot(y_attn.astype(BF16), wout_ref[0:D_ATTN, :], preferred_element_type=F32)
    mix = mix + jnp.dot(yp_ref[...], wout_ref[D_ATTN:D_ATTN + D_POOL, :],
                        preferred_element_type=F32)
    mix = mix + jnp.dot(ym_ref[...], wout_ref[D_ATTN + D_POOL:, :],
                        preferred_element_type=F32)
    x1 = x_ref[...] + mix

    h2 = _rmsnorm(x1, gffn_ref[...]).astype(BF16)
    ff = jnp.zeros(x1.shape, F32)
    for c in range(D_FF // ff_chunk):
        a = jnp.dot(h2, w1_ref[:, c * ff_chunk:(c + 1) * ff_chunk], preferred_element_type=F32)
        a = jnp.maximum(a, 0.0)
        ff = ff + jnp.dot((a * a).astype(BF16), w2_ref[c * ff_chunk:(c + 1) * ff_chunk, :],
                          preferred_element_type=F32)
    out_ref[...] = _rmsnorm(x1 + ff, gfin_ref[...])


def _out_ffn(x, o_list, lse_list, y_pool, y_mem, w_out, g_ffn, w_ff1, w_ff2, g_final,
             *, tm=512, ff_chunk=1024):
    B, S, _ = x.shape
    expand = jnp.asarray(np.repeat(np.eye(N_ATTN_HEADS, dtype=np.float32), HEAD_DIM, axis=1),
                         dtype=BF16)
    row = lambda b, t: (b, t, 0)
    tile = lambda width: pl.BlockSpec((None, tm, width), row)
    return pl.pallas_call(
        functools.partial(_out_ffn_kernel, ff_chunk=ff_chunk),
        grid=(B, S // tm),
        in_specs=[
            tile(D_MODEL),
            tile(D_ATTN), tile(D_ATTN), tile(D_ATTN),
            tile(N_ATTN_HEADS), tile(N_ATTN_HEADS), tile(N_ATTN_HEADS),
            tile(D_POOL), tile(D_XMEM),
            _resident(expand.shape),
            _resident(w_out.shape),
            _resident((1, D_MODEL)),
            _resident(w_ff1.shape),
            _resident(w_ff2.shape),
            _resident((1, D_MODEL)),
        ],
        out_specs=tile(D_MODEL),
        out_shape=jax.ShapeDtypeStruct((B, S, D_MODEL), F32),
        compiler_params=pltpu.CompilerParams(
            dimension_semantics=("parallel", "parallel"),
            vmem_limit_bytes=VMEM_LIMIT_BYTES),
    )(x, *o_list, *lse_list, y_pool, y_mem, expand, w_out, g_ffn, w_ff1, w_ff2, g_final)


def _to_residues(t, dilation):
    if dilation == 1:
        return t
    B, S, C = t.shape
    t = t.reshape(B, S // dilation, dilation, C).transpose(0, 2, 1, 3)
    return t.reshape(B * dilation, S // dilation, C)


def _from_residues(t, dilation, B):
    if dilation == 1:
        return t
    N, L, C = t.shape
    t = t.reshape(B, dilation, L, C).transpose(0, 2, 1, 3)
    return t.reshape(B, L * dilation, C)


def _block_diag(w_pool):
    G, C, E = w_pool.shape
    out = jnp.zeros((G * C, G * E), w_pool.dtype)
    for g in range(G):
        out = out.at[g * C:(g + 1) * C, g * E:(g + 1) * E].set(w_pool[g])
    return out


def kernel(x, mem, g_mix, w_in, g_mem, w_mem_kv, w_pool, pool_scale, w_out,
           g_ffn, w_ff1, w_ff2, g_final):
    B, S, _ = x.shape
    depth = w_in.shape[0]
    for i in range(depth):
        km, vm = _memkv(mem, g_mem[i][None], w_mem_kv[i].astype(BF16))
        q, k, v, qm, y_pool = _inproj(
            x, g_mix[i][None], w_in[i].astype(BF16),
            _block_diag(w_pool[i]).astype(BF16), pool_scale[i][None])

        o_list, lse_list = [], []
        for _, dilation in DILATED:
            o, lse = _band_attn(*(_to_residues(t, dilation) for t in (q, k, v)), dilation)
            o_list.append(_from_residues(o, dilation, B))
            lse_list.append(_from_residues(lse, dilation, B))
        y_mem = _mem_attn(qm, km, vm)

        assert depth == 1
        x = _out_ffn(x, o_list, lse_list, y_pool, y_mem, w_out[i].astype(BF16),
                     g_ffn[i][None], w_ff1[i].astype(BF16), w_ff2[i].astype(BF16),
                     g_final[None])
    return x
```

```python
import functools

import numpy as np
import jax
import jax.numpy as jnp
from jax import lax
from jax.experimental import pallas as pl
from jax.experimental.pallas import tpu as pltpu

D_MODEL = 1024
HEAD_DIM = 64
N_ATTN_HEADS = 8
N_MEM_HEADS = 4
POOL_WINDOWS = (2, 4, 8, 16)
POOL_GROUP_DIM = 64
D_ATTN = N_ATTN_HEADS * HEAD_DIM
D_POOL = len(POOL_WINDOWS) * POOL_GROUP_DIM
D_XMEM = N_MEM_HEADS * HEAD_DIM
DILATED = ((128, 1), (512, 4), (2048, 16))
BLK = 128
SPAN = 128
D_FF = 4 * D_MODEL
EPS = 1e-6
POOL_HALO = 16

N_RES = 16
TILE = N_RES * BLK
PAIR = 2 * HEAD_DIM
N_PAIRS = N_ATTN_HEADS // 2

BF16 = jnp.bfloat16
F32 = jnp.float32
LOG2E = 1.4426950408889634
LN2 = 0.6931471805599453

VMEM_LIMIT_BYTES = 56 * 1024 * 1024

assert all(w // d == SPAN for w, d in DILATED)
assert [d for _, d in DILATED] == [1, 4, N_RES]
assert max(POOL_WINDOWS) <= POOL_HALO


def _rmsnorm(x, g):
    ms = jnp.mean(x * x, axis=-1, keepdims=True)
    return (x * lax.rsqrt(ms + EPS)) * g


def _resident(shape):
    return pl.BlockSpec(shape, lambda *_: (0,) * len(shape), pipeline_mode=pl.Buffered(1))


def _residue_permutation(rows):
    per = rows // N_RES
    p = np.zeros((rows, rows), np.float32)
    for r in range(N_RES):
        for i in range(per):
            p[r * per + i, N_RES * i + r] = 1.0
    return p


def _memkv_kernel(mem_ref, g_ref, w_ref, km_ref, vm_ref):
    mn = _rmsnorm(mem_ref[...], g_ref[...]).astype(BF16)
    kv = jnp.dot(mn, w_ref[...], preferred_element_type=F32)
    km_ref[...] = kv[:, :D_XMEM].astype(BF16)
    vm_ref[...] = kv[:, D_XMEM:].astype(BF16)


def _memkv(mem, g_mem, w_mem_kv):
    B, M, _ = mem.shape
    return pl.pallas_call(
        _memkv_kernel,
        grid=(B,),
        in_specs=[
            pl.BlockSpec((None, M, D_MODEL), lambda b: (b, 0, 0)),
            _resident((1, D_MODEL)),
            _resident((D_MODEL, 2 * D_XMEM)),
        ],
        out_specs=[
            pl.BlockSpec((None, M, D_XMEM), lambda b: (b, 0, 0)),
            pl.BlockSpec((None, M, D_XMEM), lambda b: (b, 0, 0)),
        ],
        out_shape=[jax.ShapeDtypeStruct((B, M, D_XMEM), BF16)] * 2,
    )(mem, g_mem, w_mem_kv)


def _inproj_kernel(x_ref, g_ref, w_ref, wpool_ref, pscale_ref, perm_ref,
                   q_ref, k_ref, v_ref, qr_ref, kr_ref, vr_ref, qm_ref, yp_ref, ext_ref, *, tm):
    t = pl.program_id(1)
    h = _rmsnorm(x_ref[...], g_ref[...]).astype(BF16)

    def proj(lo, hi):
        return jnp.dot(h, w_ref[:, lo:hi], preferred_element_type=F32)

    scale = LOG2E * HEAD_DIM ** -0.5
    q = (proj(0, D_ATTN) * scale).astype(BF16)
    k = proj(D_ATTN, 2 * D_ATTN).astype(BF16)
    v = proj(2 * D_ATTN, 3 * D_ATTN).astype(BF16)
    qm_ref[...] = (proj(3 * D_ATTN + D_POOL, 3 * D_ATTN + D_POOL + D_XMEM) * scale).astype(BF16)
    perm = perm_ref[...]
    for nat, nat_ref, res_ref in ((q, q_ref, qr_ref), (k, k_ref, kr_ref), (v, v_ref, vr_ref)):
        nat_ref[...] = nat
        moved = jnp.dot(perm, nat, preferred_element_type=F32).astype(BF16)
        res_ref[...] = moved.reshape(N_RES, tm // N_RES, D_ATTN)

    u = proj(3 * D_ATTN, 3 * D_ATTN + D_POOL)

    @pl.when(t == 0)
    def _():
        ext_ref[0:POOL_HALO, :] = jnp.zeros((POOL_HALO, D_POOL), F32)

    ext_ref[POOL_HALO:POOL_HALO + tm, :] = u
    group = lax.broadcasted_iota(jnp.int32, (1, D_POOL), 1) // POOL_GROUP_DIM
    win = jnp.zeros((1, D_POOL), jnp.int32)
    for g, w in enumerate(POOL_WINDOWS):
        win = jnp.where(group == g, w, win)
    total = u
    for j in range(1, max(POOL_WINDOWS)):
        total = total + jnp.where(win > j, ext_ref[pl.ds(POOL_HALO - j, tm), :], 0.0)
    pos = t * tm + lax.broadcasted_iota(jnp.int32, (tm, 1), 0)
    cnt = jnp.minimum(win, pos + 1).astype(F32)
    d = (total / cnt - u).astype(BF16)
    yp = jnp.dot(d, wpool_ref[...], preferred_element_type=F32) * pscale_ref[...]
    yp_ref[...] = yp.astype(BF16)
    ext_ref[0:POOL_HALO, :] = ext_ref[tm:tm + POOL_HALO, :]


def _inproj(x, g_mix, w_in, wpool_bd, pool_scale, *, tm=512):
    B, S, _ = x.shape
    d_in = w_in.shape[1]
    per_tile = TILE // tm
    perm = jnp.asarray(_residue_permutation(tm), dtype=BF16)
    row = lambda b, t: (b, t, 0)
    res = lambda b, t: (b, t // per_tile, 0, t % per_tile, 0)
    nat_spec = pl.BlockSpec((None, tm, D_ATTN), row)
    res_spec = pl.BlockSpec((None, None, N_RES, tm // N_RES, D_ATTN), res)
    nat_shape = jax.ShapeDtypeStruct((B, S, D_ATTN), BF16)
    res_shape = jax.ShapeDtypeStruct((B, S // TILE, N_RES, BLK, D_ATTN), BF16)
    return pl.pallas_call(
        functools.partial(_inproj_kernel, tm=tm),
        grid=(B, S // tm),
        in_specs=[
            pl.BlockSpec((None, tm, D_MODEL), row),
            _resident((1, D_MODEL)),
            _resident((D_MODEL, d_in)),
            _resident((D_POOL, D_POOL)),
            _resident((1, D_POOL)),
            _resident((tm, tm)),
        ],
        out_specs=[
            nat_spec, nat_spec, nat_spec, res_spec, res_spec, res_spec,
            pl.BlockSpec((None, tm, D_XMEM), row),
            pl.BlockSpec((None, tm, D_POOL), row),
        ],
        out_shape=[
            nat_shape, nat_shape, nat_shape, res_shape, res_shape, res_shape,
            jax.ShapeDtypeStruct((B, S, D_XMEM), BF16),
            jax.ShapeDtypeStruct((B, S, D_POOL), BF16),
        ],
        scratch_shapes=[pltpu.VMEM((tm + POOL_HALO, D_POOL), F32)],
        compiler_params=pltpu.CompilerParams(
            dimension_semantics=("arbitrary", "arbitrary"),
            vmem_limit_bytes=VMEM_LIMIT_BYTES),
    )(x, g_mix, w_in, wpool_bd, pool_scale, perm)


def _alibi_bias_table(dilation, q_off, k_off):
    slopes = (2.0 ** (-8.0 * np.arange(1, N_ATTN_HEADS + 1, dtype=np.float32)
                      / N_ATTN_HEADS)).astype(np.float32)
    rel = q_off[:, None] - k_off[None, :]
    valid = (rel >= 0) & (rel <= SPAN)
    bias = -slopes[:, None, None] * (dilation * rel).astype(np.float32)[None] * np.float32(LOG2E)
    full = np.where(valid[None], bias, -np.inf)
    first = np.where((valid & (k_off[None, :] >= BLK))[None], bias, -np.inf)
    table = np.stack([first, full]).astype(np.float32)
    return table.reshape(2, N_PAIRS, 2 * BLK, 2 * BLK)


def _pair_scores(qp, kp):
    is_lo = lax.broadcasted_iota(jnp.int32, (1, PAIR), 1) < HEAD_DIM
    zero = jnp.zeros_like(qp)
    q2 = jnp.concatenate([jnp.where(is_lo, qp, zero), jnp.where(is_lo, zero, qp)], axis=0)
    return lax.dot_general(q2, kp, (((1,), (1,)), ((), ())), preferred_element_type=F32)


def _pair_pv(p2, vp):
    vaug = jnp.concatenate([vp, jnp.ones_like(vp)], axis=1)
    return jnp.dot(p2.astype(BF16), vaug, preferred_element_type=F32)


def _pair_select(out2):
    R = out2.shape[0] // 2
    is_lo = lax.broadcasted_iota(jnp.int32, (1, PAIR), 1) < HEAD_DIM
    num = jnp.where(is_lo, out2[:R, :PAIR], out2[R:, :PAIR])
    den = jnp.where(is_lo, out2[:R, PAIR:], out2[R:, PAIR:])
    return num, den


def _head_pair_attention(qp, kp, vp):
    s2 = _pair_scores(qp, kp)
    p2 = jnp.exp2(s2 - jnp.max(s2, axis=-1, keepdims=True))
    num, den = _pair_select(_pair_pv(p2, vp))
    return num / den


def _band_attn_kernel(q_ref, kh_ref, kc_ref, vh_ref, vc_ref, bias_ref,
                      o_ref, lse_ref, kf_ref, vf_ref, sa_ref, sb_ref, ma_ref, mb_ref, *, tq):
    i = pl.program_id(1)
    kf_ref[0:BLK, :] = kh_ref[...]
    kf_ref[BLK:, :] = kc_ref[...]
    vf_ref[0:BLK, :] = vh_ref[...]
    vf_ref[BLK:, :] = vc_ref[...]
    head_lane = lax.broadcasted_iota(jnp.int32, (1, N_ATTN_HEADS), 1)
    nsub = tq // BLK

    def scores(j, bufs):
        s_ref, m_ref = bufs
        r0 = pl.multiple_of(j * BLK, BLK)
        sel = jnp.where(jnp.logical_and(i == 0, j == 0), 0, 1)
        for hp in range(N_PAIRS):
            cols = slice(hp * PAIR, (hp + 1) * PAIR)
            s2 = _pair_scores(q_ref[pl.ds(r0, BLK), cols], kf_ref[pl.ds(r0, 2 * BLK), cols])
            s2 = s2 + bias_ref[sel, hp]
            m2 = jnp.max(s2, axis=-1, keepdims=True)
            s_ref[hp] = s2 - m2
            m_ref[hp] = jnp.broadcast_to(m2, (2 * BLK, BLK))

    def finish(j, bufs):
        s_ref, m_ref = bufs
        r0 = pl.multiple_of(j * BLK, BLK)
        row_max = jnp.zeros((BLK, N_ATTN_HEADS), F32)
        row_sum = jnp.ones((BLK, N_ATTN_HEADS), F32)
        for hp in range(N_PAIRS):
            cols = slice(hp * PAIR, (hp + 1) * PAIR)
            out2 = _pair_pv(jnp.exp2(s_ref[hp]), vf_ref[pl.ds(r0, 2 * BLK), cols])
            num, den = _pair_select(out2)
            o_ref[pl.ds(r0, BLK), cols] = (num / den).astype(BF16)
            for hh in range(2):
                rows = slice(hh * BLK, (hh + 1) * BLK)
                hit = head_lane == 2 * hp + hh
                row_max = jnp.where(hit, m_ref[hp, rows, 0:N_ATTN_HEADS], row_max)
                row_sum = jnp.where(hit, out2[rows, PAIR:PAIR + N_ATTN_HEADS], row_sum)
        lse_ref[pl.ds(r0, BLK), :] = LN2 * (row_max + jnp.log2(row_sum))

    assert nsub % 2 == 0
    buf_a, buf_b = (sa_ref, ma_ref), (sb_ref, mb_ref)
    scores(0, buf_a)

    def body(jj, carry):
        j = 2 * jj
        scores(j + 1, buf_b)
        finish(j, buf_a)
        scores(j + 2, buf_a)
        finish(j + 1, buf_b)
        return carry

    lax.fori_loop(0, nsub // 2 - 1, body, 0)
    scores(nsub - 1, buf_b)
    finish(nsub - 2, buf_a)
    finish(nsub - 1, buf_b)


def _band_attn(q, k, v, *, tq=1024):
    N, L, _ = q.shape
    sub = tq // BLK
    offs = np.arange(2 * BLK)
    bias = jnp.asarray(_alibi_bias_table(1, BLK + offs[:BLK], offs))
    cur = lambda n, i: (n, i, 0)
    halo = lambda n, i: (n, jnp.maximum(i * sub - 1, 0), 0)
    return pl.pallas_call(
        functools.partial(_band_attn_kernel, tq=tq),
        grid=(N, L // tq),
        in_specs=[
            pl.BlockSpec((None, tq, D_ATTN), cur),
            pl.BlockSpec((None, BLK, D_ATTN), halo),
            pl.BlockSpec((None, tq, D_ATTN), cur),
            pl.BlockSpec((None, BLK, D_ATTN), halo),
            pl.BlockSpec((None, tq, D_ATTN), cur),
            _resident(bias.shape),
        ],
        out_specs=[
            pl.BlockSpec((None, tq, D_ATTN), cur),
            pl.BlockSpec((None, tq, N_ATTN_HEADS), cur),
        ],
        out_shape=[
            jax.ShapeDtypeStruct((N, L, D_ATTN), BF16),
            jax.ShapeDtypeStruct((N, L, N_ATTN_HEADS), F32),
        ],
        scratch_shapes=[pltpu.VMEM((tq + BLK, D_ATTN), BF16)] * 2
        + [pltpu.VMEM((N_PAIRS, 2 * BLK, 2 * BLK), F32)] * 2
        + [pltpu.VMEM((N_PAIRS, 2 * BLK, BLK), F32)] * 2,
        compiler_params=pltpu.CompilerParams(
            dimension_semantics=("parallel", "parallel"),
            vmem_limit_bytes=VMEM_LIMIT_BYTES),
    )(q, k, k, v, v, bias)


Q4 = BLK // 4


def _dil_attn_kernel(q_ref, kp_ref, kc_ref, vp_ref, vc_ref, b16_ref, b4_ref, unperm_ref,
                     o_ref, lse_ref,
                     sa_ref, sb_ref, ma_ref, mb_ref, n16_ref, d16_ref, m16_ref, op_ref, lp_ref):
    sel_first = jnp.where(pl.program_id(1) == 0, 0, 1)
    is_lo = lax.broadcasted_iota(jnp.int32, (1, PAIR), 1) < HEAD_DIM
    lane = lax.broadcasted_iota(jnp.int32, (1, PAIR), 1)
    cols = lambda hp: slice(hp * PAIR, (hp + 1) * PAIR)

    def scores_into(bufs, q_of, k_of, bias_of):
        s_ref, m_ref = bufs
        for hp in range(N_PAIRS):
            s2 = _pair_scores(q_of(hp), k_of(hp)) + bias_of(hp)
            m2 = jnp.max(s2, axis=-1, keepdims=True)
            s_ref[hp] = s2 - m2
            m_ref[hp] = jnp.where(is_lo, m2[:BLK], m2[BLK:])

    def pv_from(bufs, v_of, hp):
        s_ref, m_ref = bufs
        num, den = _pair_select(_pair_pv(jnp.exp2(s_ref[hp]), v_of(hp)))
        return num, den, m_ref[hp]

    def group(r4, carry):
        slabs = [r4 + 4 * a for a in range(4)]

        def window16(prev_ref, cur_ref, a):
            return lambda hp: jnp.concatenate(
                [prev_ref[slabs[a], :, cols(hp)], cur_ref[slabs[a], :, cols(hp)]], axis=0)

        def window4(prev_ref, cur_ref, c):
            def rows(hp):
                parts = []
                for s in slabs:
                    if c == 0:
                        parts += [prev_ref[s, BLK - Q4:BLK, cols(hp)], cur_ref[s, 0:Q4, cols(hp)]]
                    else:
                        parts.append(cur_ref[s, (c - 1) * Q4:(c + 1) * Q4, cols(hp)])
                return jnp.concatenate(parts, axis=0)
            return rows

        def quarter_rows(ref, c):
            return lambda hp: jnp.concatenate(
                [ref[s, c * Q4:(c + 1) * Q4, cols(hp)] for s in slabs], axis=0)

        def score16(a, bufs):
            scores_into(bufs, lambda hp: q_ref[slabs[a], :, cols(hp)],
                        window16(kp_ref, kc_ref, a), lambda hp: b16_ref[sel_first, hp])

        def finish16(a, bufs):
            for hp in range(N_PAIRS):
                num, den, top = pv_from(bufs, window16(vp_ref, vc_ref, a), hp)
                n16_ref[a, :, cols(hp)] = num
                d16_ref[a, :, cols(hp)] = den
                m16_ref[a, :, cols(hp)] = top

        def score4(c, bufs):
            sel = sel_first if c == 0 else 1
            scores_into(bufs, quarter_rows(q_ref, c), window4(kp_ref, kc_ref, c),
                        lambda hp: b4_ref[sel, hp])

        def finish4(c, bufs):
            gather = lambda ref, hp: jnp.concatenate(
                [ref[a, c * Q4:(c + 1) * Q4, cols(hp)] for a in range(4)], axis=0)
            lse_tile = jnp.zeros((BLK, PAIR), F32)
            for hp in range(N_PAIRS):
                num4, den4, top4 = pv_from(bufs, window4(vp_ref, vc_ref, c), hp)
                top16 = gather(m16_ref, hp)
                top = jnp.maximum(top16, top4)
                w16 = jnp.exp2(top16 - top)
                w4 = jnp.exp2(top4 - top)
                num = w16 * gather(n16_ref, hp) + w4 * num4
                den = w16 * gather(d16_ref, hp) + w4 * den4
                o = (num / den).astype(BF16)
                lse = LN2 * (top + jnp.log2(den))
                for a, s in enumerate(slabs):
                    op_ref[s, c * Q4:(c + 1) * Q4, cols(hp)] = o[a * Q4:(a + 1) * Q4]
                lse_tile = jnp.where(lane == 2 * hp, lse, lse_tile)
                lse_tile = jnp.where(lane == 2 * hp + 1, pltpu.roll(lse, HEAD_DIM, 1), lse_tile)
            for a, s in enumerate(slabs):
                lp_ref[s, c * Q4:(c + 1) * Q4, :] = lse_tile[a * Q4:(a + 1) * Q4]

        stages = ([(functools.partial(score16, a), functools.partial(finish16, a))
                   for a in range(4)]
                  + [(functools.partial(score4, c), functools.partial(finish4, c))
                     for c in range(4)])
        bufs = ((sa_ref, ma_ref), (sb_ref, mb_ref))
        stages[0][0](bufs[0])
        for n, (_, finish) in enumerate(stages):
            if n + 1 < len(stages):
                stages[n + 1][0](bufs[(n + 1) % 2])
            finish(bufs[n % 2])
        return carry

    lax.fori_loop(0, 4, group, 0)

    unperm = unperm_ref[...]
    per = unperm.shape[0] // N_RES
    for a in range(BLK // per):
        rows = slice(a * per, (a + 1) * per)
        nat = slice(a * per * N_RES, (a + 1) * per * N_RES)
        o_res = jnp.concatenate([op_ref[r, rows, :] for r in range(N_RES)], axis=0)
        o_ref[nat, :] = jnp.dot(unperm, o_res, preferred_element_type=F32).astype(BF16)
        l_res = jnp.concatenate([lp_ref[r, rows, :] for r in range(N_RES)], axis=0)
        hi = l_res.astype(BF16)
        rest = l_res - hi.astype(F32)
        mid = rest.astype(BF16)
        low = (rest - mid.astype(F32)).astype(BF16)
        l_nat = (jnp.dot(unperm, hi, preferred_element_type=F32)
                 + jnp.dot(unperm, mid, preferred_element_type=F32)
                 + jnp.dot(unperm, low, preferred_element_type=F32))
        lse_ref[nat, :] = l_nat[:, 0:N_ATTN_HEADS]


def _dil_attn(qr, kr, vr, *, per=32):
    B, T = qr.shape[:2]
    S = T * TILE
    offs = np.arange(2 * BLK)
    b16 = jnp.asarray(_alibi_bias_table(16, BLK + offs[:BLK], offs))
    qo = np.arange(BLK)
    b4 = jnp.asarray(_alibi_bias_table(
        4, BLK + 4 * (qo % Q4) + qo // Q4, 4 * (offs % (2 * Q4)) + offs // (2 * Q4)))
    unperm = jnp.asarray(_residue_permutation(per * N_RES).T, dtype=BF16)
    slab_block = (None, None, N_RES, BLK, D_ATTN)
    cur = lambda b, t: (b, t, 0, 0, 0)
    prev = lambda b, t: (b, jnp.maximum(t - 1, 0), 0, 0, 0)
    nat = lambda b, t: (b, t, 0)
    return pl.pallas_call(
        _dil_attn_kernel,
        grid=(B, T),
        in_specs=[
            pl.BlockSpec(slab_block, cur),
            pl.BlockSpec(slab_block, prev),
            pl.BlockSpec(slab_block, cur),
            pl.BlockSpec(slab_block, prev),
            pl.BlockSpec(slab_block, cur),
            _resident(b16.shape),
            _resident(b4.shape),
            _resident(unperm.shape),
        ],
        out_specs=[
            pl.BlockSpec((None, TILE, D_ATTN), nat),
            pl.BlockSpec((None, TILE, N_ATTN_HEADS), nat),
        ],
        out_shape=[
            jax.ShapeDtypeStruct((B, S, D_ATTN), BF16),
            jax.ShapeDtypeStruct((B, S, N_ATTN_HEADS), F32),
        ],
        scratch_shapes=[pltpu.VMEM((N_PAIRS, 2 * BLK, 2 * BLK), F32)] * 2
        + [pltpu.VMEM((N_PAIRS, BLK, PAIR), F32)] * 2
        + [pltpu.VMEM((4, BLK, D_ATTN), F32)] * 3
        + [pltpu.VMEM((N_RES, BLK, D_ATTN), BF16), pltpu.VMEM((N_RES, BLK, PAIR), F32)],
        compiler_params=pltpu.CompilerParams(
            dimension_semantics=("parallel", "parallel"),
            vmem_limit_bytes=VMEM_LIMIT_BYTES),
    )(qr, kr, kr, vr, vr, b16, b4, unperm)


def _mem_attn_kernel(qm_ref, km_ref, vm_ref, y_ref, *, tm, rows):
    def body(j, carry):
        r0 = pl.multiple_of(j * rows, rows)
        for hp in range(N_MEM_HEADS // 2):
            cols = slice(hp * PAIR, (hp + 1) * PAIR)
            o_pair = _head_pair_attention(
                qm_ref[pl.ds(r0, rows), cols], km_ref[:, cols], vm_ref[:, cols])
            y_ref[pl.ds(r0, rows), cols] = o_pair.astype(BF16)
        return carry

    lax.fori_loop(0, tm // rows, body, 0)


def _mem_attn(qm, km, vm, *, tm=1024, rows=128):
    B, S, _ = qm.shape
    M = km.shape[1]
    return pl.pallas_call(
        functools.partial(_mem_attn_kernel, tm=tm, rows=rows),
        grid=(B, S // tm),
        in_specs=[
            pl.BlockSpec((None, tm, D_XMEM), lambda b, t: (b, t, 0)),
            pl.BlockSpec((None, M, D_XMEM), lambda b, t: (b, 0, 0)),
            pl.BlockSpec((None, M, D_XMEM), lambda b, t: (b, 0, 0)),
        ],
        out_specs=pl.BlockSpec((None, tm, D_XMEM), lambda b, t: (b, t, 0)),
        out_shape=jax.ShapeDtypeStruct((B, S, D_XMEM), BF16),
        compiler_params=pltpu.CompilerParams(
            dimension_semantics=("parallel", "parallel"),
            vmem_limit_bytes=VMEM_LIMIT_BYTES),
    )(qm, km, vm)


def _out_ffn_kernel(x_ref, o1_ref, oa_ref, l1_ref, la_ref,
                    yp_ref, ym_ref, expand_ref, wout_ref, gffn_ref, w1_ref, w2_ref,
                    gfin_ref, out_ref, *, ff_chunk):
    lses = [l1_ref[...], la_ref[...]]
    top = jnp.maximum(lses[0], lses[1])
    es = [jnp.exp(l - top) for l in lses]
    den = es[0] + es[1]

    def expand(w):
        hi = w.astype(BF16)
        lo = (w - hi.astype(F32)).astype(BF16)
        e = expand_ref[...]
        return (jnp.dot(hi, e, preferred_element_type=F32)
                + jnp.dot(lo, e, preferred_element_type=F32))

    y_attn = jnp.zeros(o1_ref.shape, F32)
    for e, o_ref in zip(es, (o1_ref, oa_ref)):
        y_attn = y_attn + expand(e / den) * o_ref[...].astype(F32)

    mix = jnp.dot(y_attn.astype(BF16), wout_ref[0:D_ATTN, :], preferred_element_type=F32)
    mix = mix + jnp.dot(yp_ref[...], wout_ref[D_ATTN:D_ATTN + D_POOL, :],
                        preferred_element_type=F32)
    mix = mix + jnp.dot(ym_ref[...], wout_ref[D_ATTN + D_POOL:, :],
                        preferred_element_type=F32)
    x1 = x_ref[...] + mix

    h2 = _rmsnorm(x1, gffn_ref[...]).astype(BF16)
    ff = jnp.zeros(x1.shape, F32)
    for c in range(D_FF // ff_chunk):
        a = jnp.dot(h2, w1_ref[:, c * ff_chunk:(c + 1) * ff_chunk], preferred_element_type=F32)
        a = jnp.maximum(a, 0.0)
        ff = ff + jnp.dot((a * a).astype(BF16), w2_ref[c * ff_chunk:(c + 1) * ff_chunk, :],
                          preferred_element_type=F32)
    out_ref[...] = _rmsnorm(x1 + ff, gfin_ref[...])


def _out_ffn(x, o_list, lse_list, y_pool, y_mem, w_out, g_ffn, w_ff1, w_ff2, g_final,
             *, tm=512, ff_chunk=1024):
    B, S, _ = x.shape
    expand = jnp.asarray(np.repeat(np.eye(N_ATTN_HEADS, dtype=np.float32), HEAD_DIM, axis=1),
                         dtype=BF16)
    row = lambda b, t: (b, t, 0)
    tile = lambda width: pl.BlockSpec((None, tm, width), row)
    return pl.pallas_call(
        functools.partial(_out_ffn_kernel, ff_chunk=ff_chunk),
        grid=(B, S // tm),
        in_specs=[
            tile(D_MODEL),
            tile(D_ATTN), tile(D_ATTN),
            tile(N_ATTN_HEADS), tile(N_ATTN_HEADS),
            tile(D_POOL), tile(D_XMEM),
            _resident(expand.shape),
            _resident(w_out.shape),
            _resident((1, D_MODEL)),
            _resident(w_ff1.shape),
            _resident(w_ff2.shape),
            _resident((1, D_MODEL)),
        ],
        out_specs=tile(D_MODEL),
        out_shape=jax.ShapeDtypeStruct((B, S, D_MODEL), F32),
        compiler_params=pltpu.CompilerParams(
            dimension_semantics=("parallel", "parallel"),
            vmem_limit_bytes=VMEM_LIMIT_BYTES),
    )(x, *o_list, *lse_list, y_pool, y_mem, expand, w_out, g_ffn, w_ff1, w_ff2, g_final)


def _block_diag(w_pool):
    G, C, E = w_pool.shape
    out = jnp.zeros((G * C, G * E), w_pool.dtype)
    for g in range(G):
        out = out.at[g * C:(g + 1) * C, g * E:(g + 1) * E].set(w_pool[g])
    return out


def kernel(x, mem, g_mix, w_in, g_mem, w_mem_kv, w_pool, pool_scale, w_out,
           g_ffn, w_ff1, w_ff2, g_final):
    depth = w_in.shape[0]
    for i in range(depth):
        km, vm = _memkv(mem, g_mem[i][None], w_mem_kv[i].astype(BF16))
        q, k, v, qr, kr, vr, qm, y_pool = _inproj(
            x, g_mix[i][None], w_in[i].astype(BF16),
            _block_diag(w_pool[i]).astype(BF16), pool_scale[i][None])

        o1, lse1 = _band_attn(q, k, v)
        oa, lsea = _dil_attn(qr, kr, vr)
        y_mem = _mem_attn(qm, km, vm)

        assert depth == 1
        x = _out_ffn(x, [o1, oa], [lse1, lsea], y_pool, y_mem, w_out[i].astype(BF16),
                     g_ffn[i][None], w_ff1[i].astype(BF16), w_ff2[i].astype(BF16),
                     g_final[None])
    return x
```

```python
import functools

import numpy as np
import jax
import jax.numpy as jnp
from jax import lax
from jax.experimental import pallas as pl
from jax.experimental.pallas import tpu as pltpu

D_MODEL = 1024
HEAD_DIM = 64
N_ATTN_HEADS = 8
N_MEM_HEADS = 4
POOL_WINDOWS = (2, 4, 8, 16)
POOL_GROUP_DIM = 64
D_ATTN = N_ATTN_HEADS * HEAD_DIM
D_POOL = len(POOL_WINDOWS) * POOL_GROUP_DIM
D_XMEM = N_MEM_HEADS * HEAD_DIM
DILATED = ((128, 1), (512, 4), (2048, 16))
BLK = 128
SPAN = 128
D_FF = 4 * D_MODEL
EPS = 1e-6
POOL_HALO = 16

N_RES = 16
TILE = N_RES * BLK
PAIR = 2 * HEAD_DIM
N_PAIRS = N_ATTN_HEADS // 2
N_MEM_PAIRS = N_MEM_HEADS // 2

BF16 = jnp.bfloat16
F32 = jnp.float32
LOG2E = 1.4426950408889634
LN2 = 0.6931471805599453

VMEM_LIMIT_BYTES = 56 * 1024 * 1024

assert all(w // d == SPAN for w, d in DILATED)
assert [d for _, d in DILATED] == [1, 4, N_RES]
assert max(POOL_WINDOWS) <= POOL_HALO
assert all(w & (w - 1) == 0 for w in POOL_WINDOWS) and list(POOL_WINDOWS) == sorted(POOL_WINDOWS)


def _rmsnorm(x, g):
    ms = jnp.mean(x * x, axis=-1, keepdims=True)
    return (x * lax.rsqrt(ms + EPS)) * g


def _resident(shape):
    return pl.BlockSpec(shape, lambda *_: (0,) * len(shape), pipeline_mode=pl.Buffered(1))


def _residue_permutation(rows):
    per = rows // N_RES
    p = np.zeros((rows, rows), np.float32)
    for r in range(N_RES):
        for i in range(per):
            p[r * per + i, N_RES * i + r] = 1.0
    return p


def _memkv_kernel(mem_ref, g_ref, w_ref, km_ref, vm_ref):
    mn = _rmsnorm(mem_ref[...], g_ref[...]).astype(BF16)
    kv = jnp.dot(mn, w_ref[...], preferred_element_type=F32)
    km_ref[...] = kv[:, :D_XMEM].astype(BF16)
    vm_ref[...] = kv[:, D_XMEM:].astype(BF16)


def _memkv(mem, g_mem, w_mem_kv):
    B, M, _ = mem.shape
    return pl.pallas_call(
        _memkv_kernel,
        grid=(B,),
        in_specs=[
            pl.BlockSpec((None, M, D_MODEL), lambda b: (b, 0, 0)),
            _resident((1, D_MODEL)),
            _resident((D_MODEL, 2 * D_XMEM)),
        ],
        out_specs=[
            pl.BlockSpec((None, M, D_XMEM), lambda b: (b, 0, 0)),
            pl.BlockSpec((None, M, D_XMEM), lambda b: (b, 0, 0)),
        ],
        out_shape=[jax.ShapeDtypeStruct((B, M, D_XMEM), BF16)] * 2,
    )(mem, g_mem, w_mem_kv)


def _inproj_kernel(x_ref, g_ref, w_ref, wpool_ref, pscale_ref, perm_ref,
                   q_ref, k_ref, v_ref, qr_ref, kr_ref, vr_ref, qm_ref, yp_ref, carry_ref, *, tm):
    t = pl.program_id(1)
    h = _rmsnorm(x_ref[...], g_ref[...]).astype(BF16)

    def proj(lo, hi):
        return jnp.dot(h, w_ref[:, lo:hi], preferred_element_type=F32)

    u = proj(3 * D_ATTN, 3 * D_ATTN + D_POOL)

    @pl.when(t == 0)
    def _():
        carry_ref[...] = jnp.zeros((POOL_HALO, D_POOL), F32)

    run = jnp.concatenate([carry_ref[...], u], axis=0)
    carry_ref[...] = u[tm - POOL_HALO:, :]
    group = lax.broadcasted_iota(jnp.int32, (1, D_POOL), 1) // POOL_GROUP_DIM
    total = None
    width = 1
    for g, w in enumerate(POOL_WINDOWS):
        while width < w:
            run = run + pltpu.roll(run, width, 0)
            width *= 2
        total = run if total is None else jnp.where(group >= g, run, total)
    total = total[POOL_HALO:, :]
    win = jnp.zeros((1, D_POOL), jnp.int32)
    for g, w in enumerate(POOL_WINDOWS):
        win = jnp.where(group == g, w, win)
    pos = t * tm + lax.broadcasted_iota(jnp.int32, (tm, 1), 0)
    cnt = jnp.minimum(win, pos + 1).astype(F32)
    d = (total / cnt - u).astype(BF16)
    yp = jnp.dot(d, wpool_ref[...], preferred_element_type=F32) * pscale_ref[...]
    yp_ref[...] = yp.astype(BF16)

    scale = LOG2E * HEAD_DIM ** -0.5
    q = (proj(0, D_ATTN) * scale).astype(BF16)
    k = proj(D_ATTN, 2 * D_ATTN).astype(BF16)
    v = proj(2 * D_ATTN, 3 * D_ATTN).astype(BF16)
    qm_ref[...] = (proj(3 * D_ATTN + D_POOL, 3 * D_ATTN + D_POOL + D_XMEM) * scale).astype(BF16)
    perm = perm_ref[...]
    for nat, nat_ref, res_ref in ((q, q_ref, qr_ref), (k, k_ref, kr_ref), (v, v_ref, vr_ref)):
        nat_ref[...] = nat
        moved = jnp.dot(perm, nat, preferred_element_type=F32).astype(BF16)
        res_ref[...] = moved.reshape(N_RES, tm // N_RES, D_ATTN)


def _inproj(x, g_mix, w_in, wpool_bd, pool_scale, *, tm=512):
    B, S, _ = x.shape
    d_in = w_in.shape[1]
    per_tile = TILE // tm
    perm = jnp.asarray(_residue_permutation(tm), dtype=BF16)
    row = lambda b, t: (b, t, 0)
    res = lambda b, t: (b, t // per_tile, 0, t % per_tile, 0)
    nat_spec = pl.BlockSpec((None, tm, D_ATTN), row)
    res_spec = pl.BlockSpec((None, None, N_RES, tm // N_RES, D_ATTN), res)
    nat_shape = jax.ShapeDtypeStruct((B, S, D_ATTN), BF16)
    res_shape = jax.ShapeDtypeStruct((B, S // TILE, N_RES, BLK, D_ATTN), BF16)
    return pl.pallas_call(
        functools.partial(_inproj_kernel, tm=tm),
        grid=(B, S // tm),
        in_specs=[
            pl.BlockSpec((None, tm, D_MODEL), row),
            _resident((1, D_MODEL)),
            _resident((D_MODEL, d_in)),
            _resident((D_POOL, D_POOL)),
            _resident((1, D_POOL)),
            _resident((tm, tm)),
        ],
        out_specs=[
            nat_spec, nat_spec, nat_spec, res_spec, res_spec, res_spec,
            pl.BlockSpec((None, tm, D_XMEM), row),
            pl.BlockSpec((None, tm, D_POOL), row),
        ],
        out_shape=[
            nat_shape, nat_shape, nat_shape, res_shape, res_shape, res_shape,
            jax.ShapeDtypeStruct((B, S, D_XMEM), BF16),
            jax.ShapeDtypeStruct((B, S, D_POOL), BF16),
        ],
        scratch_shapes=[pltpu.VMEM((POOL_HALO, D_POOL), F32)],
        compiler_params=pltpu.CompilerParams(
            dimension_semantics=("arbitrary", "arbitrary"),
            vmem_limit_bytes=VMEM_LIMIT_BYTES),
    )(x, g_mix, w_in, wpool_bd, pool_scale, perm)


def _alibi_bias_table(dilation, q_off, k_off):
    slopes = (2.0 ** (-8.0 * np.arange(1, N_ATTN_HEADS + 1, dtype=np.float32)
                      / N_ATTN_HEADS)).astype(np.float32)
    rel = q_off[:, None] - k_off[None, :]
    valid = (rel >= 0) & (rel <= SPAN)
    bias = -slopes[:, None, None] * (dilation * rel).astype(np.float32)[None] * np.float32(LOG2E)
    full = np.where(valid[None], bias, -np.inf)
    first = np.where((valid & (k_off[None, :] >= BLK))[None], bias, -np.inf)
    table = np.stack([first, full]).astype(np.float32)
    return table.reshape(2, N_PAIRS, 2 * BLK, 2 * BLK)


def _pair_scores(qp, kp):
    is_lo = lax.broadcasted_iota(jnp.int32, (1, PAIR), 1) < HEAD_DIM
    zero = jnp.zeros_like(qp)
    q2 = jnp.concatenate([jnp.where(is_lo, qp, zero), jnp.where(is_lo, zero, qp)], axis=0)
    return lax.dot_general(q2, kp, (((1,), (1,)), ((), ())), preferred_element_type=F32)


def _pair_pv(p2, vp):
    vaug = jnp.concatenate([vp, jnp.ones_like(vp)], axis=1)
    return jnp.dot(p2.astype(BF16), vaug, preferred_element_type=F32)


def _pair_select(out2):
    R = out2.shape[0] // 2
    is_lo = lax.broadcasted_iota(jnp.int32, (1, PAIR), 1) < HEAD_DIM
    num = jnp.where(is_lo, out2[:R, :PAIR], out2[R:, :PAIR])
    den = jnp.where(is_lo, out2[:R, PAIR:], out2[R:, PAIR:])
    return num, den


def _band_attn_kernel(q_ref, kh_ref, kc_ref, vh_ref, vc_ref, bias_ref, qm_ref, km_ref, vm_ref,
                      o_ref, lse_ref, ym_ref,
                      kf_ref, vf_ref, sa_ref, sb_ref, ma_ref, mb_ref, *, tq):
    i = pl.program_id(1)
    kf_ref[0:BLK, :] = kh_ref[...]
    kf_ref[BLK:, :] = kc_ref[...]
    vf_ref[0:BLK, :] = vh_ref[...]
    vf_ref[BLK:, :] = vc_ref[...]
    head_lane = lax.broadcasted_iota(jnp.int32, (1, N_ATTN_HEADS), 1)
    nsub = tq // BLK

    def scores(j, bufs):
        s_ref, m_ref = bufs
        r0 = pl.multiple_of(j * BLK, BLK)
        sel = jnp.where(jnp.logical_and(i == 0, j == 0), 0, 1)
        for hp in range(N_PAIRS):
            cols = slice(hp * PAIR, (hp + 1) * PAIR)
            s2 = _pair_scores(q_ref[pl.ds(r0, BLK), cols], kf_ref[pl.ds(r0, 2 * BLK), cols])
            s2 = s2 + bias_ref[sel, hp]
            m2 = jnp.max(s2, axis=-1, keepdims=True)
            s_ref[hp] = s2 - m2
            m_ref[hp] = jnp.broadcast_to(m2, (2 * BLK, BLK))
        for mp in range(N_MEM_PAIRS):
            cols = slice(mp * PAIR, (mp + 1) * PAIR)
            s2 = _pair_scores(qm_ref[pl.ds(r0, BLK), cols], km_ref[:, cols])
            s_ref[N_PAIRS + mp] = s2 - jnp.max(s2, axis=-1, keepdims=True)

    def finish(j, bufs):
        s_ref, m_ref = bufs
        r0 = pl.multiple_of(j * BLK, BLK)
        row_max = jnp.zeros((BLK, N_ATTN_HEADS), F32)
        row_sum = jnp.ones((BLK, N_ATTN_HEADS), F32)
        for hp in range(N_PAIRS):
            cols = slice(hp * PAIR, (hp + 1) * PAIR)
            out2 = _pair_pv(jnp.exp2(s_ref[hp]), vf_ref[pl.ds(r0, 2 * BLK), cols])
            num, den = _pair_select(out2)
            o_ref[pl.ds(r0, BLK), cols] = (num / den).astype(BF16)
            for hh in range(2):
                rows = slice(hh * BLK, (hh + 1) * BLK)
                hit = head_lane == 2 * hp + hh
                row_max = jnp.where(hit, m_ref[hp, rows, 0:N_ATTN_HEADS], row_max)
                row_sum = jnp.where(hit, out2[rows, PAIR:PAIR + N_ATTN_HEADS], row_sum)
        lse_ref[pl.ds(r0, BLK), :] = LN2 * (row_max + jnp.log2(row_sum))
        for mp in range(N_MEM_PAIRS):
            cols = slice(mp * PAIR, (mp + 1) * PAIR)
            num, den = _pair_select(_pair_pv(jnp.exp2(s_ref[N_PAIRS + mp]), vm_ref[:, cols]))
            ym_ref[pl.ds(r0, BLK), cols] = (num / den).astype(BF16)

    assert nsub % 2 == 0
    buf_a, buf_b = (sa_ref, ma_ref), (sb_ref, mb_ref)
    scores(0, buf_a)

    def body(jj, carry):
        j = 2 * jj
        scores(j + 1, buf_b)
        finish(j, buf_a)
        scores(j + 2, buf_a)
        finish(j + 1, buf_b)
        return carry

    lax.fori_loop(0, nsub // 2 - 1, body, 0)
    scores(nsub - 1, buf_b)
    finish(nsub - 2, buf_a)
    finish(nsub - 1, buf_b)


def _band_attn(q, k, v, qm, km, vm, *, tq=1024):
    N, L, _ = q.shape
    M = km.shape[1]
    assert M == 2 * BLK
    sub = tq // BLK
    offs = np.arange(2 * BLK)
    bias = jnp.asarray(_alibi_bias_table(1, BLK + offs[:BLK], offs))
    cur = lambda n, i: (n, i, 0)
    halo = lambda n, i: (n, jnp.maximum(i * sub - 1, 0), 0)
    return pl.pallas_call(
        functools.partial(_band_attn_kernel, tq=tq),
        grid=(N, L // tq),
        in_specs=[
            pl.BlockSpec((None, tq, D_ATTN), cur),
            pl.BlockSpec((None, BLK, D_ATTN), halo),
            pl.BlockSpec((None, tq, D_ATTN), cur),
            pl.BlockSpec((None, BLK, D_ATTN), halo),
            pl.BlockSpec((None, tq, D_ATTN), cur),
            _resident(bias.shape),
            pl.BlockSpec((None, tq, D_XMEM), cur),
            pl.BlockSpec((None, M, D_XMEM), lambda n, i: (n, 0, 0)),
            pl.BlockSpec((None, M, D_XMEM), lambda n, i: (n, 0, 0)),
        ],
        out_specs=[
            pl.BlockSpec((None, tq, D_ATTN), cur),
            pl.BlockSpec((None, tq, N_ATTN_HEADS), cur),
            pl.BlockSpec((None, tq, D_XMEM), cur),
        ],
        out_shape=[
            jax.ShapeDtypeStruct((N, L, D_ATTN), BF16),
            jax.ShapeDtypeStruct((N, L, N_ATTN_HEADS), F32),
            jax.ShapeDtypeStruct((N, L, D_XMEM), BF16),
        ],
        scratch_shapes=[pltpu.VMEM((tq + BLK, D_ATTN), BF16)] * 2
        + [pltpu.VMEM((N_PAIRS + N_MEM_PAIRS, 2 * BLK, 2 * BLK), F32)] * 2
        + [pltpu.VMEM((N_PAIRS, 2 * BLK, BLK), F32)] * 2,
        compiler_params=pltpu.CompilerParams(
            dimension_semantics=("parallel", "parallel"),
            vmem_limit_bytes=VMEM_LIMIT_BYTES),
    )(q, k, k, v, v, bias, qm, km, vm)


Q4 = BLK // 4


def _dil_attn_kernel(q_ref, kp_ref, kc_ref, vp_ref, vc_ref, b16_ref, b4_ref, unperm_ref,
                     o_ref, lse_ref,
                     sa_ref, sb_ref, ma_ref, mb_ref, n16_ref, d16_ref, m16_ref, op_ref, lp_ref):
    sel_first = jnp.where(pl.program_id(1) == 0, 0, 1)
    is_lo = lax.broadcasted_iota(jnp.int32, (1, PAIR), 1) < HEAD_DIM
    lane = lax.broadcasted_iota(jnp.int32, (1, PAIR), 1)
    cols = lambda hp: slice(hp * PAIR, (hp + 1) * PAIR)

    def scores_into(bufs, q_of, k_of, bias_of):
        s_ref, m_ref = bufs
        for hp in range(N_PAIRS):
            s2 = _pair_scores(q_of(hp), k_of(hp)) + bias_of(hp)
            m2 = jnp.max(s2, axis=-1, keepdims=True)
            s_ref[hp] = s2 - m2
            m_ref[hp] = jnp.where(is_lo, m2[:BLK], m2[BLK:])

    def pv_from(bufs, v_of, hp):
        s_ref, m_ref = bufs
        num, den = _pair_select(_pair_pv(jnp.exp2(s_ref[hp]), v_of(hp)))
        return num, den, m_ref[hp]

    def group(r4, carry):
        slabs = [r4 + 4 * a for a in range(4)]

        def window16(prev_ref, cur_ref, a):
            return lambda hp: jnp.concatenate(
                [prev_ref[slabs[a], :, cols(hp)], cur_ref[slabs[a], :, cols(hp)]], axis=0)

        def window4(prev_ref, cur_ref, c):
            def rows(hp):
                parts = []
                for s in slabs:
                    if c == 0:
                        parts += [prev_ref[s, BLK - Q4:BLK, cols(hp)], cur_ref[s, 0:Q4, cols(hp)]]
                    else:
                        parts.append(cur_ref[s, (c - 1) * Q4:(c + 1) * Q4, cols(hp)])
                return jnp.concatenate(parts, axis=0)
            return rows

        def quarter_rows(ref, c):
            return lambda hp: jnp.concatenate(
                [ref[s, c * Q4:(c + 1) * Q4, cols(hp)] for s in slabs], axis=0)

        def score16(a, bufs):
            scores_into(bufs, lambda hp: q_ref[slabs[a], :, cols(hp)],
                        window16(kp_ref, kc_ref, a), lambda hp: b16_ref[sel_first, hp])

        def finish16(a, bufs):
            for hp in range(N_PAIRS):
                num, den, top = pv_from(bufs, window16(vp_ref, vc_ref, a), hp)
                n16_ref[a, :, cols(hp)] = num
                d16_ref[a, :, cols(hp)] = den
                m16_ref[a, :, cols(hp)] = top

        def score4(c, bufs):
            sel = sel_first if c == 0 else 1
            scores_into(bufs, quarter_rows(q_ref, c), window4(kp_ref, kc_ref, c),
                        lambda hp: b4_ref[sel, hp])

        def finish4(c, bufs):
            gather = lambda ref, hp: jnp.concatenate(
                [ref[a, c * Q4:(c + 1) * Q4, cols(hp)] for a in range(4)], axis=0)
            lse_tile = jnp.zeros((BLK, PAIR), F32)
            for hp in range(N_PAIRS):
                num4, den4, top4 = pv_from(bufs, window4(vp_ref, vc_ref, c), hp)
                top16 = gather(m16_ref, hp)
                top = jnp.maximum(top16, top4)
                w16 = jnp.exp2(top16 - top)
                w4 = jnp.exp2(top4 - top)
                num = w16 * gather(n16_ref, hp) + w4 * num4
                den = w16 * gather(d16_ref, hp) + w4 * den4
                o = (num / den).astype(BF16)
                lse = LN2 * (top + jnp.log2(den))
                for a, s in enumerate(slabs):
                    op_ref[s, c * Q4:(c + 1) * Q4, cols(hp)] = o[a * Q4:(a + 1) * Q4]
                lse_tile = jnp.where(lane == 2 * hp, lse, lse_tile)
                lse_tile = jnp.where(lane == 2 * hp + 1, pltpu.roll(lse, HEAD_DIM, 1), lse_tile)
            for a, s in enumerate(slabs):
                lp_ref[s, c * Q4:(c + 1) * Q4, :] = lse_tile[a * Q4:(a + 1) * Q4]

        stages = ([(functools.partial(score16, a), functools.partial(finish16, a))
                   for a in range(4)]
                  + [(functools.partial(score4, c), functools.partial(finish4, c))
                     for c in range(4)])
        bufs = ((sa_ref, ma_ref), (sb_ref, mb_ref))
        stages[0][0](bufs[0])
        for n, (_, finish) in enumerate(stages):
            if n + 1 < len(stages):
                stages[n + 1][0](bufs[(n + 1) % 2])
            finish(bufs[n % 2])
        return carry

    lax.fori_loop(0, 4, group, 0)

    unperm = unperm_ref[...]
    per = unperm.shape[0] // N_RES
    for a in range(BLK // per):
        rows = slice(a * per, (a + 1) * per)
        nat = slice(a * per * N_RES, (a + 1) * per * N_RES)
        o_res = jnp.concatenate([op_ref[r, rows, :] for r in range(N_RES)], axis=0)
        o_ref[nat, :] = jnp.dot(unperm, o_res, preferred_element_type=F32).astype(BF16)
        l_res = jnp.concatenate([lp_ref[r, rows, :] for r in range(N_RES)], axis=0)
        hi = l_res.astype(BF16)
        rest = l_res - hi.astype(F32)
        mid = rest.astype(BF16)
        low = (rest - mid.astype(F32)).astype(BF16)
        l_nat = (jnp.dot(unperm, hi, preferred_element_type=F32)
                 + jnp.dot(unperm, mid, preferred_element_type=F32)
                 + jnp.dot(unperm, low, preferred_element_type=F32))
        lse_ref[nat, :] = l_nat[:, 0:N_ATTN_HEADS]


def _dil_attn(qr, kr, vr, *, per=32):
    B, T = qr.shape[:2]
    S = T * TILE
    offs = np.arange(2 * BLK)
    b16 = jnp.asarray(_alibi_bias_table(16, BLK + offs[:BLK], offs))
    qo = np.arange(BLK)
    b4 = jnp.asarray(_alibi_bias_table(
        4, BLK + 4 * (qo % Q4) + qo // Q4, 4 * (offs % (2 * Q4)) + offs // (2 * Q4)))
    unperm = jnp.asarray(_residue_permutation(per * N_RES).T, dtype=BF16)
    slab_block = (None, None, N_RES, BLK, D_ATTN)
    cur = lambda b, t: (b, t, 0, 0, 0)
    prev = lambda b, t: (b, jnp.maximum(t - 1, 0), 0, 0, 0)
    nat = lambda b, t: (b, t, 0)
    return pl.pallas_call(
        _dil_attn_kernel,
        grid=(B, T),
        in_specs=[
            pl.BlockSpec(slab_block, cur),
            pl.BlockSpec(slab_block, prev),
            pl.BlockSpec(slab_block, cur),
            pl.BlockSpec(slab_block, prev),
            pl.BlockSpec(slab_block, cur),
            _resident(b16.shape),
            _resident(b4.shape),
            _resident(unperm.shape),
        ],
        out_specs=[
            pl.BlockSpec((None, TILE, D_ATTN), nat),
            pl.BlockSpec((None, TILE, N_ATTN_HEADS), nat),
        ],
        out_shape=[
            jax.ShapeDtypeStruct((B, S, D_ATTN), BF16),
            jax.ShapeDtypeStruct((B, S, N_ATTN_HEADS), F32),
        ],
        scratch_shapes=[pltpu.VMEM((N_PAIRS, 2 * BLK, 2 * BLK), F32)] * 2
        + [pltpu.VMEM((N_PAIRS, BLK, PAIR), F32)] * 2
        + [pltpu.VMEM((4, BLK, D_ATTN), F32)] * 3
        + [pltpu.VMEM((N_RES, BLK, D_ATTN), BF16), pltpu.VMEM((N_RES, BLK, PAIR), F32)],
        compiler_params=pltpu.CompilerParams(
            dimension_semantics=("parallel", "parallel"),
            vmem_limit_bytes=VMEM_LIMIT_BYTES),
    )(qr, kr, kr, vr, vr, b16, b4, unperm)


def _out_ffn_kernel(x_ref, o1_ref, oa_ref, l1_ref, la_ref,
                    yp_ref, ym_ref, expand_ref, wout_ref, gffn_ref, w1_ref, w2_ref,
                    gfin_ref, out_ref, *, ff_chunk):
    lses = [l1_ref[...], la_ref[...]]
    top = jnp.maximum(lses[0], lses[1])
    es = [jnp.exp(l - top) for l in lses]
    den = es[0] + es[1]

    def expand(w):
        hi = w.astype(BF16)
        lo = (w - hi.astype(F32)).astype(BF16)
        e = expand_ref[...]
        return (jnp.dot(hi, e, preferred_element_type=F32)
                + jnp.dot(lo, e, preferred_element_type=F32))

    y_attn = jnp.zeros(o1_ref.shape, F32)
    for e, o_ref in zip(es, (o1_ref, oa_ref)):
        y_attn = y_attn + expand(e / den) * o_ref[...].astype(F32)

    mix = jnp.dot(y_attn.astype(BF16), wout_ref[0:D_ATTN, :], preferred_element_type=F32)
    mix = mix + jnp.dot(yp_ref[...], wout_ref[D_ATTN:D_ATTN + D_POOL, :],
                        preferred_element_type=F32)
    mix = mix + jnp.dot(ym_ref[...], wout_ref[D_ATTN + D_POOL:, :],
                        preferred_element_type=F32)
    x1 = x_ref[...] + mix

    h2 = _rmsnorm(x1, gffn_ref[...]).astype(BF16)
    ff = jnp.zeros(x1.shape, F32)
    for c in range(D_FF // ff_chunk):
        a = jnp.dot(h2, w1_ref[:, c * ff_chunk:(c + 1) * ff_chunk], preferred_element_type=F32)
        a = jnp.maximum(a, 0.0)
        ff = ff + jnp.dot((a * a).astype(BF16), w2_ref[c * ff_chunk:(c + 1) * ff_chunk, :],
                          preferred_element_type=F32)
    out_ref[...] = _rmsnorm(x1 + ff, gfin_ref[...])


def _out_ffn(x, o_list, lse_list, y_pool, y_mem, w_out, g_ffn, w_ff1, w_ff2, g_final,
             *, tm=512, ff_chunk=1024):
    B, S, _ = x.shape
    expand = jnp.asarray(np.repeat(np.eye(N_ATTN_HEADS, dtype=np.float32), HEAD_DIM, axis=1),
                         dtype=BF16)
    row = lambda b, t: (b, t, 0)
    tile = lambda width: pl.BlockSpec((None, tm, width), row)
    return pl.pallas_call(
        functools.partial(_out_ffn_kernel, ff_chunk=ff_chunk),
        grid=(B, S // tm),
        in_specs=[
            tile(D_MODEL),
            tile(D_ATTN), tile(D_ATTN),
            tile(N_ATTN_HEADS), tile(N_ATTN_HEADS),
            tile(D_POOL), tile(D_XMEM),
            _resident(expand.shape),
            _resident(w_out.shape),
            _resident((1, D_MODEL)),
            _resident(w_ff1.shape),
            _resident(w_ff2.shape),
            _resident((1, D_MODEL)),
        ],
        out_specs=tile(D_MODEL),
        out_shape=jax.ShapeDtypeStruct((B, S, D_MODEL), F32),
        compiler_params=pltpu.CompilerParams(
            dimension_semantics=("parallel", "parallel"),
            vmem_limit_bytes=VMEM_LIMIT_BYTES),
    )(x, *o_list, *lse_list, y_pool, y_mem, expand, w_out, g_ffn, w_ff1, w_ff2, g_final)


def _block_diag(w_pool):
    G, C, E = w_pool.shape
    out = jnp.zeros((G * C, G * E), w_pool.dtype)
    for g in range(G):
        out = out.at[g * C:(g + 1) * C, g * E:(g + 1) * E].set(w_pool[g])
    return out


def kernel(x, mem, g_mix, w_in, g_mem, w_mem_kv, w_pool, pool_scale, w_out,
           g_ffn, w_ff1, w_ff2, g_final):
    depth = w_in.shape[0]
    for i in range(depth):
        km, vm = _memkv(mem, g_mem[i][None], w_mem_kv[i].astype(BF16))
        q, k, v, qr, kr, vr, qm, y_pool = _inproj(
            x, g_mix[i][None], w_in[i].astype(BF16),
            _block_diag(w_pool[i]).astype(BF16), pool_scale[i][None])

        o1, lse1, y_mem = _band_attn(q, k, v, qm, km, vm)
        oa, lsea = _dil_attn(qr, kr, vr)

        assert depth == 1
        x = _out_ffn(x, [o1, oa], [lse1, lsea], y_pool, y_mem, w_out[i].astype(BF16),
                     g_ffn[i][None], w_ff1[i].astype(BF16), w_ff2[i].astype(BF16),
                     g_final[None])
    return x
```

```python
import functools

import numpy as np
import jax
import jax.numpy as jnp
from jax import lax
from jax.experimental import pallas as pl
from jax.experimental.pallas import tpu as pltpu

D_MODEL = 1024
HEAD_DIM = 64
N_ATTN_HEADS = 8
N_MEM_HEADS = 4
POOL_WINDOWS = (2, 4, 8, 16)
POOL_GROUP_DIM = 64
D_ATTN = N_ATTN_HEADS * HEAD_DIM
D_POOL = len(POOL_WINDOWS) * POOL_GROUP_DIM
D_XMEM = N_MEM_HEADS * HEAD_DIM
DILATED = ((128, 1), (512, 4), (2048, 16))
BLK = 128
SPAN = 128
D_FF = 4 * D_MODEL
EPS = 1e-6
POOL_HALO = 16

N_RES = 16
TILE = N_RES * BLK
PAIR = 2 * HEAD_DIM
LANES = 128
N_PAIRS = N_ATTN_HEADS // 2
N_MEM_PAIRS = N_MEM_HEADS // 2

BF16 = jnp.bfloat16
F32 = jnp.float32
LOG2E = 1.4426950408889634
LN2 = 0.6931471805599453

VMEM_LIMIT_BYTES = 56 * 1024 * 1024

assert all(w // d == SPAN for w, d in DILATED)
assert [d for _, d in DILATED] == [1, 4, N_RES]
assert max(POOL_WINDOWS) <= POOL_HALO
assert all(w & (w - 1) == 0 for w in POOL_WINDOWS) and list(POOL_WINDOWS) == sorted(POOL_WINDOWS)


def _rmsnorm(x, g):
    ms = jnp.mean(x * x, axis=-1, keepdims=True)
    return (x * lax.rsqrt(ms + EPS)) * g


def _resident(shape):
    return pl.BlockSpec(shape, lambda *_: (0,) * len(shape), pipeline_mode=pl.Buffered(1))


def _residue_permutation(rows):
    per = rows // N_RES
    p = np.zeros((rows, rows), np.float32)
    for r in range(N_RES):
        for i in range(per):
            p[r * per + i, N_RES * i + r] = 1.0
    return p


def _memkv_kernel(mem_ref, g_ref, w_ref, km_ref, vm_ref):
    mn = _rmsnorm(mem_ref[...], g_ref[...]).astype(BF16)
    kv = jnp.dot(mn, w_ref[...], preferred_element_type=F32)
    km_ref[...] = kv[:, :D_XMEM].astype(BF16)
    vm_ref[...] = kv[:, D_XMEM:].astype(BF16)


def _memkv(mem, g_mem, w_mem_kv):
    B, M, _ = mem.shape
    return pl.pallas_call(
        _memkv_kernel,
        grid=(B,),
        in_specs=[
            pl.BlockSpec((None, M, D_MODEL), lambda b: (b, 0, 0)),
            _resident((1, D_MODEL)),
            _resident((D_MODEL, 2 * D_XMEM)),
        ],
        out_specs=[
            pl.BlockSpec((None, M, D_XMEM), lambda b: (b, 0, 0)),
            pl.BlockSpec((None, M, D_XMEM), lambda b: (b, 0, 0)),
        ],
        out_shape=[jax.ShapeDtypeStruct((B, M, D_XMEM), BF16)] * 2,
    )(mem, g_mem, w_mem_kv)


def _inproj_kernel(x_ref, g_ref, w_ref, wpool_ref, pscale_ref, perm_ref,
                   q_ref, k_ref, v_ref, qr_ref, kr_ref, vr_ref, qm_ref, yp_ref,
                   carry_ref, h_ref, *, tm):
    t = pl.program_id(1)
    h_ref[...] = _rmsnorm(x_ref[...], g_ref[...]).astype(BF16)

    def proj(lo, hi):
        return jnp.dot(h_ref[...], w_ref[:, lo:hi], preferred_element_type=F32)

    u = proj(3 * D_ATTN, 3 * D_ATTN + D_POOL)

    @pl.when(t == 0)
    def _():
        carry_ref[...] = jnp.zeros((POOL_HALO, D_POOL), F32)

    run = jnp.concatenate([carry_ref[...], u], axis=0)
    carry_ref[...] = u[tm - POOL_HALO:, :]
    group = lax.broadcasted_iota(jnp.int32, (1, D_POOL), 1) // POOL_GROUP_DIM
    total = None
    width = 1
    for g, w in enumerate(POOL_WINDOWS):
        while width < w:
            run = run + pltpu.roll(run, width, 0)
            width *= 2
        total = run if total is None else jnp.where(group >= g, run, total)
    total = total[POOL_HALO:, :]
    win = jnp.zeros((1, D_POOL), jnp.int32)
    for g, w in enumerate(POOL_WINDOWS):
        win = jnp.where(group == g, w, win)
    pos = t * tm + lax.broadcasted_iota(jnp.int32, (tm, 1), 0)
    cnt = jnp.minimum(win, pos + 1).astype(F32)
    d = (total / cnt - u).astype(BF16)

    scale = LOG2E * HEAD_DIM ** -0.5
    q_ref[...] = (proj(0, D_ATTN) * scale).astype(BF16)
    k_ref[...] = proj(D_ATTN, 2 * D_ATTN).astype(BF16)
    v_ref[...] = proj(2 * D_ATTN, 3 * D_ATTN).astype(BF16)
    qm_ref[...] = (proj(3 * D_ATTN + D_POOL, 3 * D_ATTN + D_POOL + D_XMEM) * scale).astype(BF16)

    yp = jnp.dot(d, wpool_ref[...], preferred_element_type=F32) * pscale_ref[...]
    yp_ref[...] = yp.astype(BF16)

    perm = perm_ref[...]
    for nat_ref, res_ref in ((q_ref, qr_ref), (k_ref, kr_ref), (v_ref, vr_ref)):
        moved = jnp.dot(perm, nat_ref[...], preferred_element_type=F32).astype(BF16)
        res_ref[...] = moved.reshape(N_RES, tm // N_RES, D_ATTN)


def _inproj(x, g_mix, w_in, wpool_bd, pool_scale, *, tm=512):
    B, S, _ = x.shape
    d_in = w_in.shape[1]
    per_tile = TILE // tm
    perm = jnp.asarray(_residue_permutation(tm), dtype=BF16)
    row = lambda b, t: (b, t, 0)
    res = lambda b, t: (b, t // per_tile, 0, t % per_tile, 0)
    nat_spec = pl.BlockSpec((None, tm, D_ATTN), row)
    res_spec = pl.BlockSpec((None, None, N_RES, tm // N_RES, D_ATTN), res)
    nat_shape = jax.ShapeDtypeStruct((B, S, D_ATTN), BF16)
    res_shape = jax.ShapeDtypeStruct((B, S // TILE, N_RES, BLK, D_ATTN), BF16)
    return pl.pallas_call(
        functools.partial(_inproj_kernel, tm=tm),
        grid=(B, S // tm),
        in_specs=[
            pl.BlockSpec((None, tm, D_MODEL), row),
            _resident((1, D_MODEL)),
            _resident((D_MODEL, d_in)),
            _resident((D_POOL, D_POOL)),
            _resident((1, D_POOL)),
            _resident((tm, tm)),
        ],
        out_specs=[
            nat_spec, nat_spec, nat_spec, res_spec, res_spec, res_spec,
            pl.BlockSpec((None, tm, D_XMEM), row),
            pl.BlockSpec((None, tm, D_POOL), row),
        ],
        out_shape=[
            nat_shape, nat_shape, nat_shape, res_shape, res_shape, res_shape,
            jax.ShapeDtypeStruct((B, S, D_XMEM), BF16),
            jax.ShapeDtypeStruct((B, S, D_POOL), BF16),
        ],
        scratch_shapes=[pltpu.VMEM((POOL_HALO, D_POOL), F32),
                        pltpu.VMEM((tm, D_MODEL), BF16)],
        compiler_params=pltpu.CompilerParams(
            dimension_semantics=("arbitrary", "arbitrary"),
            vmem_limit_bytes=VMEM_LIMIT_BYTES),
    )(x, g_mix, w_in, wpool_bd, pool_scale, perm)


def _alibi_bias_table(dilation, q_off, k_off):
    slopes = (2.0 ** (-8.0 * np.arange(1, N_ATTN_HEADS + 1, dtype=np.float32)
                      / N_ATTN_HEADS)).astype(np.float32)
    rel = q_off[:, None] - k_off[None, :]
    valid = (rel >= 0) & (rel <= SPAN)
    bias = -slopes[:, None, None] * (dilation * rel).astype(np.float32)[None] * np.float32(LOG2E)
    full = np.where(valid[None], bias, -np.inf)
    first = np.where((valid & (k_off[None, :] >= BLK))[None], bias, -np.inf)
    table = np.stack([first, full]).astype(np.float32)
    return table.reshape(2, N_PAIRS, 2 * BLK, 2 * BLK)


def _pair_scores(qp, kp):
    is_lo = lax.broadcasted_iota(jnp.int32, (1, PAIR), 1) < HEAD_DIM
    zero = jnp.zeros_like(qp)
    q2 = jnp.concatenate([jnp.where(is_lo, qp, zero), jnp.where(is_lo, zero, qp)], axis=0)
    return lax.dot_general(q2, kp, (((1,), (1,)), ((), ())), preferred_element_type=F32)


def _pair_pv(p2, vp):
    vaug = jnp.concatenate([vp, jnp.ones_like(vp)], axis=1)
    return jnp.dot(p2.astype(BF16), vaug, preferred_element_type=F32)


def _pair_select(out2):
    R = out2.shape[0] // 2
    is_lo = lax.broadcasted_iota(jnp.int32, (1, PAIR), 1) < HEAD_DIM
    num = jnp.where(is_lo, out2[:R, :PAIR], out2[R:, :PAIR])
    den = jnp.where(is_lo, out2[:R, PAIR:], out2[R:, PAIR:])
    return num, den


def _band_attn_kernel(q_ref, kh_ref, kc_ref, vh_ref, vc_ref, bias_ref, qm_ref, km_ref, vm_ref,
                      o_ref, lse_ref, ym_ref,
                      kf_ref, vf_ref, sa_ref, sb_ref, ma_ref, mb_ref, sta_ref, stb_ref, *, tq):
    i = pl.program_id(1)
    kf_ref[0:BLK, :] = kh_ref[...]
    kf_ref[BLK:, :] = kc_ref[...]
    vf_ref[0:BLK, :] = vh_ref[...]
    vf_ref[BLK:, :] = vc_ref[...]
    nsub = tq // BLK

    def scores(j, bufs):
        s_ref, m_ref, _ = bufs
        r0 = pl.multiple_of(j * BLK, BLK)
        sel = jnp.where(jnp.logical_and(i == 0, j == 0), 0, 1)
        for hp in range(N_PAIRS):
            cols = slice(hp * PAIR, (hp + 1) * PAIR)
            s2 = _pair_scores(q_ref[pl.ds(r0, BLK), cols], kf_ref[pl.ds(r0, 2 * BLK), cols])
            s2 = s2 + bias_ref[sel, hp]
            m2 = jnp.max(s2, axis=-1, keepdims=True)
            s_ref[hp] = s2 - m2
            m_ref[hp] = jnp.broadcast_to(m2, (2 * BLK, BLK))
        for mp in range(N_MEM_PAIRS):
            cols = slice(mp * PAIR, (mp + 1) * PAIR)
            s2 = _pair_scores(qm_ref[pl.ds(r0, BLK), cols], km_ref[:, cols])
            s_ref[N_PAIRS + mp] = s2 - jnp.max(s2, axis=-1, keepdims=True)

    def finish(j, bufs):
        s_ref, m_ref, st_ref = bufs
        r0 = pl.multiple_of(j * BLK, BLK)
        for hp in range(N_PAIRS):
            cols = slice(hp * PAIR, (hp + 1) * PAIR)
            out2 = _pair_pv(jnp.exp2(s_ref[hp]), vf_ref[pl.ds(r0, 2 * BLK), cols])
            num, den = _pair_select(out2)
            o_ref[pl.ds(r0, BLK), cols] = (num / den).astype(BF16)
            for hh in range(2):
                rows = slice(hh * BLK, (hh + 1) * BLK)
                h = 2 * hp + hh
                st_ref[0, :, h:h + 1] = m_ref[hp, rows, h:h + 1]
                st_ref[1, :, h:h + 1] = out2[rows, PAIR + h:PAIR + h + 1]
        lse_ref[pl.ds(r0, BLK), :] = LN2 * (st_ref[0, :, 0:N_ATTN_HEADS]
                                            + jnp.log2(st_ref[1, :, 0:N_ATTN_HEADS]))
        for mp in range(N_MEM_PAIRS):
            cols = slice(mp * PAIR, (mp + 1) * PAIR)
            num, den = _pair_select(_pair_pv(jnp.exp2(s_ref[N_PAIRS + mp]), vm_ref[:, cols]))
            ym_ref[pl.ds(r0, BLK), cols] = (num / den).astype(BF16)

    assert nsub % 2 == 0
    buf_a, buf_b = (sa_ref, ma_ref, sta_ref), (sb_ref, mb_ref, stb_ref)
    scores(0, buf_a)

    def body(jj, carry):
        j = 2 * jj
        scores(j + 1, buf_b)
        finish(j, buf_a)
        scores(j + 2, buf_a)
        finish(j + 1, buf_b)
        return carry

    lax.fori_loop(0, nsub // 2 - 1, body, 0)
    scores(nsub - 1, buf_b)
    finish(nsub - 2, buf_a)
    finish(nsub - 1, buf_b)


def _band_attn(q, k, v, qm, km, vm, *, tq=1024):
    N, L, _ = q.shape
    M = km.shape[1]
    assert M == 2 * BLK
    sub = tq // BLK
    offs = np.arange(2 * BLK)
    bias = jnp.asarray(_alibi_bias_table(1, BLK + offs[:BLK], offs))
    cur = lambda n, i: (n, i, 0)
    halo = lambda n, i: (n, jnp.maximum(i * sub - 1, 0), 0)
    return pl.pallas_call(
        functools.partial(_band_attn_kernel, tq=tq),
        grid=(N, L // tq),
        in_specs=[
            pl.BlockSpec((None, tq, D_ATTN), cur),
            pl.BlockSpec((None, BLK, D_ATTN), halo),
            pl.BlockSpec((None, tq, D_ATTN), cur),
            pl.BlockSpec((None, BLK, D_ATTN), halo),
            pl.BlockSpec((None, tq, D_ATTN), cur),
            _resident(bias.shape),
            pl.BlockSpec((None, tq, D_XMEM), cur),
            pl.BlockSpec((None, M, D_XMEM), lambda n, i: (n, 0, 0)),
            pl.BlockSpec((None, M, D_XMEM), lambda n, i: (n, 0, 0)),
        ],
        out_specs=[
            pl.BlockSpec((None, tq, D_ATTN), cur),
            pl.BlockSpec((None, tq, N_ATTN_HEADS), cur),
            pl.BlockSpec((None, tq, D_XMEM), cur),
        ],
        out_shape=[
            jax.ShapeDtypeStruct((N, L, D_ATTN), BF16),
            jax.ShapeDtypeStruct((N, L, N_ATTN_HEADS), F32),
            jax.ShapeDtypeStruct((N, L, D_XMEM), BF16),
        ],
        scratch_shapes=[pltpu.VMEM((tq + BLK, D_ATTN), BF16)] * 2
        + [pltpu.VMEM((N_PAIRS + N_MEM_PAIRS, 2 * BLK, 2 * BLK), F32)] * 2
        + [pltpu.VMEM((N_PAIRS, 2 * BLK, BLK), F32)] * 2
        + [pltpu.VMEM((2, BLK, LANES), F32)] * 2,
        compiler_params=pltpu.CompilerParams(
            dimension_semantics=("parallel", "parallel"),
            vmem_limit_bytes=VMEM_LIMIT_BYTES),
    )(q, k, k, v, v, bias, qm, km, vm)


Q4 = BLK // 4


def _dil_attn_kernel(q_ref, kp_ref, kc_ref, vp_ref, vc_ref, b16_ref, b4_ref, unperm_ref,
                     o_ref, lse_ref,
                     sa_ref, sb_ref, ma_ref, mb_ref, n16_ref, d16_ref, m16_ref, op_ref, lp_ref):
    sel_first = jnp.where(pl.program_id(1) == 0, 0, 1)
    is_lo = lax.broadcasted_iota(jnp.int32, (1, PAIR), 1) < HEAD_DIM
    lane = lax.broadcasted_iota(jnp.int32, (1, PAIR), 1)
    cols = lambda hp: slice(hp * PAIR, (hp + 1) * PAIR)

    def scores_into(bufs, q_of, k_of, bias_of):
        s_ref, m_ref = bufs
        for hp in range(N_PAIRS):
            s2 = _pair_scores(q_of(hp), k_of(hp)) + bias_of(hp)
            m2 = jnp.max(s2, axis=-1, keepdims=True)
            s_ref[hp] = s2 - m2
            m_ref[hp] = jnp.where(is_lo, m2[:BLK], m2[BLK:])

    def pv_from(bufs, v_of, hp):
        s_ref, m_ref = bufs
        num, den = _pair_select(_pair_pv(jnp.exp2(s_ref[hp]), v_of(hp)))
        return num, den, m_ref[hp]

    def group(r4, carry):
        slabs = [r4 + 4 * a for a in range(4)]

        def window16(prev_ref, cur_ref, a):
            return lambda hp: jnp.concatenate(
                [prev_ref[slabs[a], :, cols(hp)], cur_ref[slabs[a], :, cols(hp)]], axis=0)

        def window4(prev_ref, cur_ref, c):
            def rows(hp):
                parts = []
                for s in slabs:
                    if c == 0:
                        parts += [prev_ref[s, BLK - Q4:BLK, cols(hp)], cur_ref[s, 0:Q4, cols(hp)]]
                    else:
                        parts.append(cur_ref[s, (c - 1) * Q4:(c + 1) * Q4, cols(hp)])
                return jnp.concatenate(parts, axis=0)
            return rows

        def quarter_rows(ref, c):
            return lambda hp: jnp.concatenate(
                [ref[s, c * Q4:(c + 1) * Q4, cols(hp)] for s in slabs], axis=0)

        def score16(a, bufs):
            scores_into(bufs, lambda hp: q_ref[slabs[a], :, cols(hp)],
                        window16(kp_ref, kc_ref, a), lambda hp: b16_ref[sel_first, hp])

        def finish16(a, bufs):
            for hp in range(N_PAIRS):
                num, den, top = pv_from(bufs, window16(vp_ref, vc_ref, a), hp)
                n16_ref[a, :, cols(hp)] = num
                d16_ref[a, :, cols(hp)] = den
                m16_ref[a, :, cols(hp)] = top

        def score4(c, bufs):
            sel = sel_first if c == 0 else 1
            scores_into(bufs, quarter_rows(q_ref, c), window4(kp_ref, kc_ref, c),
                        lambda hp: b4_ref[sel, hp])

        def finish4(c, bufs):
            gather = lambda ref, hp: jnp.concatenate(
                [ref[a, c * Q4:(c + 1) * Q4, cols(hp)] for a in range(4)], axis=0)
            lse_tile = jnp.zeros((BLK, PAIR), F32)
            for hp in range(N_PAIRS):
                num4, den4, top4 = pv_from(bufs, window4(vp_ref, vc_ref, c), hp)
                top16 = gather(m16_ref, hp)
                top = jnp.maximum(top16, top4)
                w16 = jnp.exp2(top16 - top)
                w4 = jnp.exp2(top4 - top)
                num = w16 * gather(n16_ref, hp) + w4 * num4
                den = w16 * gather(d16_ref, hp) + w4 * den4
                o = (num / den).astype(BF16)
                lse = LN2 * (top + jnp.log2(den))
                for a, s in enumerate(slabs):
                    op_ref[s, c * Q4:(c + 1) * Q4, cols(hp)] = o[a * Q4:(a + 1) * Q4]
                at = N_ATTN_HEADS * c + 2 * hp
                lse_tile = jnp.where(lane == at, lse, lse_tile)
                lse_tile = jnp.where(lane == at + 1, pltpu.roll(lse, HEAD_DIM, 1), lse_tile)
            for a, s in enumerate(slabs):
                part = lse_tile[a * Q4:(a + 1) * Q4]
                lp_ref[s] = part if c == 0 else lp_ref[s] + part

        stages = ([(functools.partial(score16, a), functools.partial(finish16, a))
                   for a in range(4)]
                  + [(functools.partial(score4, c), functools.partial(finish4, c))
                     for c in range(4)])
        bufs = ((sa_ref, ma_ref), (sb_ref, mb_ref))
        stages[0][0](bufs[0])
        for n, (_, finish) in enumerate(stages):
            if n + 1 < len(stages):
                stages[n + 1][0](bufs[(n + 1) % 2])
            finish(bufs[n % 2])
        return carry

    lax.fori_loop(0, 4, group, 0)

    unperm = unperm_ref[...]
    assert unperm.shape[0] == N_RES * Q4
    l_res = jnp.concatenate([lp_ref[r] for r in range(N_RES)], axis=0)
    hi = l_res.astype(BF16)
    rest = l_res - hi.astype(F32)
    mid = rest.astype(BF16)
    low = (rest - mid.astype(F32)).astype(BF16)
    l_nat = (jnp.dot(unperm, hi, preferred_element_type=F32)
             + jnp.dot(unperm, mid, preferred_element_type=F32)
             + jnp.dot(unperm, low, preferred_element_type=F32))
    for c in range(BLK // Q4):
        rows = slice(c * Q4, (c + 1) * Q4)
        nat = slice(c * Q4 * N_RES, (c + 1) * Q4 * N_RES)
        o_res = jnp.concatenate([op_ref[r, rows, :] for r in range(N_RES)], axis=0)
        o_ref[nat, :] = jnp.dot(unperm, o_res, preferred_element_type=F32).astype(BF16)
        lse_ref[nat, :] = l_nat[:, N_ATTN_HEADS * c:N_ATTN_HEADS * (c + 1)]


def _dil_attn(qr, kr, vr):
    B, T = qr.shape[:2]
    S = T * TILE
    offs = np.arange(2 * BLK)
    b16 = jnp.asarray(_alibi_bias_table(16, BLK + offs[:BLK], offs))
    qo = np.arange(BLK)
    b4 = jnp.asarray(_alibi_bias_table(
        4, BLK + 4 * (qo % Q4) + qo // Q4, 4 * (offs % (2 * Q4)) + offs // (2 * Q4)))
    unperm = jnp.asarray(_residue_permutation(Q4 * N_RES).T, dtype=BF16)
    slab_block = (None, None, N_RES, BLK, D_ATTN)
    cur = lambda b, t: (b, t, 0, 0, 0)
    prev = lambda b, t: (b, jnp.maximum(t - 1, 0), 0, 0, 0)
    nat = lambda b, t: (b, t, 0)
    return pl.pallas_call(
        _dil_attn_kernel,
        grid=(B, T),
        in_specs=[
            pl.BlockSpec(slab_block, cur),
            pl.BlockSpec(slab_block, prev),
            pl.BlockSpec(slab_block, cur),
            pl.BlockSpec(slab_block, prev),
            pl.BlockSpec(slab_block, cur),
            _resident(b16.shape),
            _resident(b4.shape),
            _resident(unperm.shape),
        ],
        out_specs=[
            pl.BlockSpec((None, TILE, D_ATTN), nat),
            pl.BlockSpec((None, TILE, N_ATTN_HEADS), nat),
        ],
        out_shape=[
            jax.ShapeDtypeStruct((B, S, D_ATTN), BF16),
            jax.ShapeDtypeStruct((B, S, N_ATTN_HEADS), F32),
        ],
        scratch_shapes=[pltpu.VMEM((N_PAIRS, 2 * BLK, 2 * BLK), F32)] * 2
        + [pltpu.VMEM((N_PAIRS, BLK, PAIR), F32)] * 2
        + [pltpu.VMEM((4, BLK, D_ATTN), F32)] * 3
        + [pltpu.VMEM((N_RES, BLK, D_ATTN), BF16), pltpu.VMEM((N_RES, Q4, PAIR), F32)],
        compiler_params=pltpu.CompilerParams(
            dimension_semantics=("parallel", "parallel"),
            vmem_limit_bytes=VMEM_LIMIT_BYTES),
    )(qr, kr, kr, vr, vr, b16, b4, unperm)


def _out_ffn_kernel(x_ref, o1_ref, oa_ref, l1_ref, la_ref,
                    yp_ref, ym_ref, expand_ref, wout_ref, gffn_ref, w1_ref, w2_ref,
                    gfin_ref, out_ref, *, sub, ff_chunk):
    def rows_of(rows):
        l1, la = l1_ref[rows, :], la_ref[rows, :]
        top = jnp.maximum(l1, la)
        e1, ea = jnp.exp(l1 - top), jnp.exp(la - top)
        w1 = e1 / (e1 + ea)
        hi = w1.astype(BF16)
        lo = (w1 - hi.astype(F32)).astype(BF16)
        w1 = (jnp.dot(hi, expand_ref[...], preferred_element_type=F32)
              + jnp.dot(lo, expand_ref[...], preferred_element_type=F32))
        oa = oa_ref[rows, :].astype(F32)
        y_attn = oa + w1 * (o1_ref[rows, :].astype(F32) - oa)

        mix = jnp.dot(y_attn.astype(BF16), wout_ref[0:D_ATTN, :], preferred_element_type=F32)
        mix = mix + jnp.dot(yp_ref[rows, :], wout_ref[D_ATTN:D_ATTN + D_POOL, :],
                            preferred_element_type=F32)
        mix = mix + jnp.dot(ym_ref[rows, :], wout_ref[D_ATTN + D_POOL:, :],
                            preferred_element_type=F32)
        x1 = x_ref[rows, :] + mix

        h2 = _rmsnorm(x1, gffn_ref[...]).astype(BF16)
        ff = jnp.zeros(x1.shape, F32)
        for c in range(D_FF // ff_chunk):
            a = jnp.dot(h2, w1_ref[:, c * ff_chunk:(c + 1) * ff_chunk],
                        preferred_element_type=F32)
            a = jnp.maximum(a, 0.0)
            ff = ff + jnp.dot((a * a).astype(BF16), w2_ref[c * ff_chunk:(c + 1) * ff_chunk, :],
                              preferred_element_type=F32)
        out_ref[rows, :] = _rmsnorm(x1 + ff, gfin_ref[...])

    for i in range(x_ref.shape[0] // sub):
        rows_of(slice(i * sub, (i + 1) * sub))


def _out_ffn(x, o_list, lse_list, y_pool, y_mem, w_out, g_ffn, w_ff1, w_ff2, g_final,
             *, tm=1024, sub=512, ff_chunk=1024):
    B, S, _ = x.shape
    expand = jnp.asarray(np.repeat(np.eye(N_ATTN_HEADS, dtype=np.float32), HEAD_DIM, axis=1),
                         dtype=BF16)
    row = lambda b, t: (b, t, 0)
    tile = lambda width: pl.BlockSpec((None, tm, width), row)
    return pl.pallas_call(
        functools.partial(_out_ffn_kernel, sub=sub, ff_chunk=ff_chunk),
        grid=(B, S // tm),
        in_specs=[
            tile(D_MODEL),
            tile(D_ATTN), tile(D_ATTN),
            tile(N_ATTN_HEADS), tile(N_ATTN_HEADS),
            tile(D_POOL), tile(D_XMEM),
            _resident(expand.shape),
            _resident(w_out.shape),
            _resident((1, D_MODEL)),
            _resident(w_ff1.shape),
            _resident(w_ff2.shape),
            _resident((1, D_MODEL)),
        ],
        out_specs=tile(D_MODEL),
        out_shape=jax.ShapeDtypeStruct((B, S, D_MODEL), F32),
        compiler_params=pltpu.CompilerParams(
            dimension_semantics=("parallel", "parallel"),
            vmem_limit_bytes=VMEM_LIMIT_BYTES),
    )(x, *o_list, *lse_list, y_pool, y_mem, expand, w_out, g_ffn, w_ff1, w_ff2, g_final)


def _block_diag(w_pool):
    G, C, E = w_pool.shape
    out = jnp.zeros((G * C, G * E), w_pool.dtype)
    for g in range(G):
        out = out.at[g * C:(g + 1) * C, g * E:(g + 1) * E].set(w_pool[g])
    return out


def kernel(x, mem, g_mix, w_in, g_mem, w_mem_kv, w_pool, pool_scale, w_out,
           g_ffn, w_ff1, w_ff2, g_final):
    depth = w_in.shape[0]
    for i in range(depth):
        km, vm = _memkv(mem, g_mem[i][None], w_mem_kv[i].astype(BF16))
        q, k, v, qr, kr, vr, qm, y_pool = _inproj(
            x, g_mix[i][None], w_in[i].astype(BF16),
            _block_diag(w_pool[i]).astype(BF16), pool_scale[i][None])

        o1, lse1, y_mem = _band_attn(q, k, v, qm, km, vm)
        oa, lsea = _dil_attn(qr, kr, vr)

        assert depth == 1
        x = _out_ffn(x, [o1, oa], [lse1, lsea], y_pool, y_mem, w_out[i].astype(BF16),
                     g_ffn[i][None], w_ff1[i].astype(BF16), w_ff2[i].astype(BF16),
                     g_final[None])
    return x
```

```python
import functools

import numpy as np
import jax
import jax.numpy as jnp
from jax import lax
from jax.experimental import pallas as pl
from jax.experimental.pallas import tpu as pltpu

D_MODEL = 1024
HEAD_DIM = 64
N_ATTN_HEADS = 8
N_MEM_HEADS = 4
POOL_WINDOWS = (2, 4, 8, 16)
POOL_GROUP_DIM = 64
D_ATTN = N_ATTN_HEADS * HEAD_DIM
D_POOL = len(POOL_WINDOWS) * POOL_GROUP_DIM
D_XMEM = N_MEM_HEADS * HEAD_DIM
DILATED = ((128, 1), (512, 4), (2048, 16))
BLK = 128
SPAN = 128
D_FF = 4 * D_MODEL
EPS = 1e-6
POOL_HALO = 16

N_RES = 16
TILE = N_RES * BLK
PAIR = 2 * HEAD_DIM
LANES = 128
N_PAIRS = N_ATTN_HEADS // 2
N_MEM_PAIRS = N_MEM_HEADS // 2

BF16 = jnp.bfloat16
F32 = jnp.float32
LOG2E = 1.4426950408889634
LN2 = 0.6931471805599453

VMEM_LIMIT_BYTES = 56 * 1024 * 1024

assert all(w // d == SPAN for w, d in DILATED)
assert [d for _, d in DILATED] == [1, 4, N_RES]
assert max(POOL_WINDOWS) <= POOL_HALO
assert all(w & (w - 1) == 0 for w in POOL_WINDOWS) and list(POOL_WINDOWS) == sorted(POOL_WINDOWS)


def _rmsnorm(x, g):
    ms = jnp.mean(x * x, axis=-1, keepdims=True)
    return (x * lax.rsqrt(ms + EPS)) * g


def _resident(shape):
    return pl.BlockSpec(shape, lambda *_: (0,) * len(shape), pipeline_mode=pl.Buffered(1))


def _residue_permutation(rows):
    per = rows // N_RES
    p = np.zeros((rows, rows), np.float32)
    for r in range(N_RES):
        for i in range(per):
            p[r * per + i, N_RES * i + r] = 1.0
    return p


def _memkv_kernel(mem_ref, g_ref, w_ref, km_ref, vm_ref):
    mn = _rmsnorm(mem_ref[...], g_ref[...]).astype(BF16)
    kv = jnp.dot(mn, w_ref[...], preferred_element_type=F32)
    km_ref[...] = kv[:, :D_XMEM].astype(BF16)
    vm_ref[...] = kv[:, D_XMEM:].astype(BF16)


def _memkv(mem, g_mem, w_mem_kv):
    B, M, _ = mem.shape
    return pl.pallas_call(
        _memkv_kernel,
        grid=(B,),
        in_specs=[
            pl.BlockSpec((None, M, D_MODEL), lambda b: (b, 0, 0)),
            _resident((1, D_MODEL)),
            _resident((D_MODEL, 2 * D_XMEM)),
        ],
        out_specs=[
            pl.BlockSpec((None, M, D_XMEM), lambda b: (b, 0, 0)),
            pl.BlockSpec((None, M, D_XMEM), lambda b: (b, 0, 0)),
        ],
        out_shape=[jax.ShapeDtypeStruct((B, M, D_XMEM), BF16)] * 2,
    )(mem, g_mem, w_mem_kv)


def _inproj_kernel(x_ref, g_ref, w_ref, wpool_ref, pscale_ref, perm_ref,
                   q_ref, k_ref, v_ref, qr_ref, kr_ref, vr_ref, qm_ref, yp_ref,
                   carry_ref, h_ref, *, tm, sub):
    t = pl.program_id(1)
    per = sub // N_RES

    @pl.when(t == 0)
    def _():
        carry_ref[...] = jnp.zeros((POOL_HALO, D_POOL), F32)

    def rows_of(i):
        rows = slice(i * sub, (i + 1) * sub)
        h_ref[rows, :] = _rmsnorm(x_ref[rows, :], g_ref[...]).astype(BF16)

        def proj(lo, hi):
            return jnp.dot(h_ref[rows, :], w_ref[:, lo:hi], preferred_element_type=F32)

        u = proj(3 * D_ATTN, 3 * D_ATTN + D_POOL)
        run = jnp.concatenate([carry_ref[...], u], axis=0)
        carry_ref[...] = u[sub - POOL_HALO:, :]
        group = lax.broadcasted_iota(jnp.int32, (1, D_POOL), 1) // POOL_GROUP_DIM
        total = None
        width = 1
        for g, w in enumerate(POOL_WINDOWS):
            while width < w:
                run = run + pltpu.roll(run, width, 0)
                width *= 2
            total = run if total is None else jnp.where(group >= g, run, total)
        total = total[POOL_HALO:, :]
        win = jnp.zeros((1, D_POOL), jnp.int32)
        for g, w in enumerate(POOL_WINDOWS):
            win = jnp.where(group == g, w, win)
        pos = t * tm + i * sub + lax.broadcasted_iota(jnp.int32, (sub, 1), 0)
        cnt = jnp.minimum(win, pos + 1).astype(F32)
        d = (total / cnt - u).astype(BF16)

        scale = LOG2E * HEAD_DIM ** -0.5
        q_ref[rows, :] = (proj(0, D_ATTN) * scale).astype(BF16)
        k_ref[rows, :] = proj(D_ATTN, 2 * D_ATTN).astype(BF16)
        v_ref[rows, :] = proj(2 * D_ATTN, 3 * D_ATTN).astype(BF16)
        qm_ref[rows, :] = (proj(3 * D_ATTN + D_POOL, 3 * D_ATTN + D_POOL + D_XMEM)
                           * scale).astype(BF16)

        yp = jnp.dot(d, wpool_ref[...], preferred_element_type=F32) * pscale_ref[...]
        yp_ref[rows, :] = yp.astype(BF16)

        perm = perm_ref[...]
        for nat_ref, res_ref in ((q_ref, qr_ref), (k_ref, kr_ref), (v_ref, vr_ref)):
            moved = jnp.dot(perm, nat_ref[rows, :], preferred_element_type=F32).astype(BF16)
            res_ref[:, i * per:(i + 1) * per, :] = moved.reshape(N_RES, per, D_ATTN)

    for i in range(tm // sub):
        rows_of(i)


def _inproj(x, g_mix, w_in, wpool_bd, pool_scale, *, tm=1024, sub=512):
    B, S, _ = x.shape
    d_in = w_in.shape[1]
    per_tile = TILE // tm
    perm = jnp.asarray(_residue_permutation(sub), dtype=BF16)
    row = lambda b, t: (b, t, 0)
    res = lambda b, t: (b, t // per_tile, 0, t % per_tile, 0)
    nat_spec = pl.BlockSpec((None, tm, D_ATTN), row)
    res_spec = pl.BlockSpec((None, None, N_RES, tm // N_RES, D_ATTN), res)
    nat_shape = jax.ShapeDtypeStruct((B, S, D_ATTN), BF16)
    res_shape = jax.ShapeDtypeStruct((B, S // TILE, N_RES, BLK, D_ATTN), BF16)
    return pl.pallas_call(
        functools.partial(_inproj_kernel, tm=tm, sub=sub),
        grid=(B, S // tm),
        in_specs=[
            pl.BlockSpec((None, tm, D_MODEL), row),
            _resident((1, D_MODEL)),
            _resident((D_MODEL, d_in)),
            _resident((D_POOL, D_POOL)),
            _resident((1, D_POOL)),
            _resident((sub, sub)),
        ],
        out_specs=[
            nat_spec, nat_spec, nat_spec, res_spec, res_spec, res_spec,
            pl.BlockSpec((None, tm, D_XMEM), row),
            pl.BlockSpec((None, tm, D_POOL), row),
        ],
        out_shape=[
            nat_shape, nat_shape, nat_shape, res_shape, res_shape, res_shape,
            jax.ShapeDtypeStruct((B, S, D_XMEM), BF16),
            jax.ShapeDtypeStruct((B, S, D_POOL), BF16),
        ],
        scratch_shapes=[pltpu.VMEM((POOL_HALO, D_POOL), F32),
                        pltpu.VMEM((tm, D_MODEL), BF16)],
        compiler_params=pltpu.CompilerParams(
            dimension_semantics=("arbitrary", "arbitrary"),
            vmem_limit_bytes=VMEM_LIMIT_BYTES),
    )(x, g_mix, w_in, wpool_bd, pool_scale, perm)


def _alibi_bias_table(dilation, q_off, k_off):
    slopes = (2.0 ** (-8.0 * np.arange(1, N_ATTN_HEADS + 1, dtype=np.float32)
                      / N_ATTN_HEADS)).astype(np.float32)
    rel = q_off[:, None] - k_off[None, :]
    valid = (rel >= 0) & (rel <= SPAN)
    bias = -slopes[:, None, None] * (dilation * rel).astype(np.float32)[None] * np.float32(LOG2E)
    full = np.where(valid[None], bias, -np.inf)
    first = np.where((valid & (k_off[None, :] >= BLK))[None], bias, -np.inf)
    table = np.stack([first, full]).astype(np.float32)
    return table.reshape(2, N_PAIRS, 2 * BLK, 2 * BLK)


def _pair_scores(qp, kp):
    is_lo = lax.broadcasted_iota(jnp.int32, (1, PAIR), 1) < HEAD_DIM
    zero = jnp.zeros_like(qp)
    q2 = jnp.concatenate([jnp.where(is_lo, qp, zero), jnp.where(is_lo, zero, qp)], axis=0)
    return lax.dot_general(q2, kp, (((1,), (1,)), ((), ())), preferred_element_type=F32)


def _pair_pv(p2, vp):
    vaug = jnp.concatenate([vp, jnp.ones_like(vp)], axis=1)
    return jnp.dot(p2.astype(BF16), vaug, preferred_element_type=F32)


def _pair_select(out2):
    R = out2.shape[0] // 2
    is_lo = lax.broadcasted_iota(jnp.int32, (1, PAIR), 1) < HEAD_DIM
    num = jnp.where(is_lo, out2[:R, :PAIR], out2[R:, :PAIR])
    den = jnp.where(is_lo, out2[:R, PAIR:], out2[R:, PAIR:])
    return num, den


def _band_attn_kernel(q_ref, kh_ref, kc_ref, vh_ref, vc_ref, bias_ref, qm_ref, km_ref, vm_ref,
                      o_ref, lse_ref, ym_ref,
                      kf_ref, vf_ref, sa_ref, sb_ref, ma_ref, mb_ref, sta_ref, stb_ref, *, tq):
    i = pl.program_id(1)
    kf_ref[0:BLK, :] = kh_ref[...]
    kf_ref[BLK:, :] = kc_ref[...]
    vf_ref[0:BLK, :] = vh_ref[...]
    vf_ref[BLK:, :] = vc_ref[...]
    nsub = tq // BLK

    def scores(j, bufs):
        s_ref, m_ref, _ = bufs
        r0 = pl.multiple_of(j * BLK, BLK)
        sel = jnp.where(jnp.logical_and(i == 0, j == 0), 0, 1)
        for hp in range(N_PAIRS):
            cols = slice(hp * PAIR, (hp + 1) * PAIR)
            s2 = _pair_scores(q_ref[pl.ds(r0, BLK), cols], kf_ref[pl.ds(r0, 2 * BLK), cols])
            s2 = s2 + bias_ref[sel, hp]
            m2 = jnp.max(s2, axis=-1, keepdims=True)
            s_ref[hp] = s2 - m2
            m_ref[hp] = jnp.broadcast_to(m2, (2 * BLK, BLK))
        for mp in range(N_MEM_PAIRS):
            cols = slice(mp * PAIR, (mp + 1) * PAIR)
            s2 = _pair_scores(qm_ref[pl.ds(r0, BLK), cols], km_ref[:, cols])
            s_ref[N_PAIRS + mp] = s2 - jnp.max(s2, axis=-1, keepdims=True)

    def finish(j, bufs):
        s_ref, m_ref, st_ref = bufs
        r0 = pl.multiple_of(j * BLK, BLK)
        for hp in range(N_PAIRS):
            cols = slice(hp * PAIR, (hp + 1) * PAIR)
            out2 = _pair_pv(jnp.exp2(s_ref[hp]), vf_ref[pl.ds(r0, 2 * BLK), cols])
            num, den = _pair_select(out2)
            o_ref[pl.ds(r0, BLK), cols] = (num / den).astype(BF16)
            for hh in range(2):
                rows = slice(hh * BLK, (hh + 1) * BLK)
                h = 2 * hp + hh
                st_ref[0, :, h:h + 1] = m_ref[hp, rows, h:h + 1]
                st_ref[1, :, h:h + 1] = out2[rows, PAIR + h:PAIR + h + 1]
        lse_ref[pl.ds(r0, BLK), :] = LN2 * (st_ref[0, :, 0:N_ATTN_HEADS]
                                            + jnp.log2(st_ref[1, :, 0:N_ATTN_HEADS]))
        for mp in range(N_MEM_PAIRS):
            cols = slice(mp * PAIR, (mp + 1) * PAIR)
            num, den = _pair_select(_pair_pv(jnp.exp2(s_ref[N_PAIRS + mp]), vm_ref[:, cols]))
            ym_ref[pl.ds(r0, BLK), cols] = (num / den).astype(BF16)

    assert nsub % 2 == 0
    buf_a, buf_b = (sa_ref, ma_ref, sta_ref), (sb_ref, mb_ref, stb_ref)
    scores(0, buf_a)

    def body(jj, carry):
        j = 2 * jj
        scores(j + 1, buf_b)
        finish(j, buf_a)
        scores(j + 2, buf_a)
        finish(j + 1, buf_b)
        return carry

    lax.fori_loop(0, nsub // 2 - 1, body, 0)
    scores(nsub - 1, buf_b)
    finish(nsub - 2, buf_a)
    finish(nsub - 1, buf_b)


def _band_attn(q, k, v, qm, km, vm, *, tq=2048):
    N, L, _ = q.shape
    M = km.shape[1]
    assert M == 2 * BLK
    sub = tq // BLK
    offs = np.arange(2 * BLK)
    bias = jnp.asarray(_alibi_bias_table(1, BLK + offs[:BLK], offs))
    cur = lambda n, i: (n, i, 0)
    halo = lambda n, i: (n, jnp.maximum(i * sub - 1, 0), 0)
    return pl.pallas_call(
        functools.partial(_band_attn_kernel, tq=tq),
        grid=(N, L // tq),
        in_specs=[
            pl.BlockSpec((None, tq, D_ATTN), cur),
            pl.BlockSpec((None, BLK, D_ATTN), halo),
            pl.BlockSpec((None, tq, D_ATTN), cur),
            pl.BlockSpec((None, BLK, D_ATTN), halo),
            pl.BlockSpec((None, tq, D_ATTN), cur),
            _resident(bias.shape),
            pl.BlockSpec((None, tq, D_XMEM), cur),
            pl.BlockSpec((None, M, D_XMEM), lambda n, i: (n, 0, 0)),
            pl.BlockSpec((None, M, D_XMEM), lambda n, i: (n, 0, 0)),
        ],
        out_specs=[
            pl.BlockSpec((None, tq, D_ATTN), cur),
            pl.BlockSpec((None, tq, N_ATTN_HEADS), cur),
            pl.BlockSpec((None, tq, D_XMEM), cur),
        ],
        out_shape=[
            jax.ShapeDtypeStruct((N, L, D_ATTN), BF16),
            jax.ShapeDtypeStruct((N, L, N_ATTN_HEADS), F32),
            jax.ShapeDtypeStruct((N, L, D_XMEM), BF16),
        ],
        scratch_shapes=[pltpu.VMEM((tq + BLK, D_ATTN), BF16)] * 2
        + [pltpu.VMEM((N_PAIRS + N_MEM_PAIRS, 2 * BLK, 2 * BLK), F32)] * 2
        + [pltpu.VMEM((N_PAIRS, 2 * BLK, BLK), F32)] * 2
        + [pltpu.VMEM((2, BLK, LANES), F32)] * 2,
        compiler_params=pltpu.CompilerParams(
            dimension_semantics=("parallel", "parallel"),
            vmem_limit_bytes=VMEM_LIMIT_BYTES),
    )(q, k, k, v, v, bias, qm, km, vm)


Q4 = BLK // 4


def _dil_attn_kernel(q_ref, kp_ref, kc_ref, vp_ref, vc_ref, b16_ref, b4_ref, unperm_ref,
                     o_ref, lse_ref,
                     sa_ref, sb_ref, ma_ref, mb_ref, n16_ref, d16_ref, m16_ref, op_ref, lp_ref):
    sel_first = jnp.where(pl.program_id(1) == 0, 0, 1)
    is_lo = lax.broadcasted_iota(jnp.int32, (1, PAIR), 1) < HEAD_DIM
    lane = lax.broadcasted_iota(jnp.int32, (1, PAIR), 1)
    cols = lambda hp: slice(hp * PAIR, (hp + 1) * PAIR)

    def scores_into(bufs, q_of, k_of, bias_of):
        s_ref, m_ref = bufs
        for hp in range(N_PAIRS):
            s2 = _pair_scores(q_of(hp), k_of(hp)) + bias_of(hp)
            m2 = jnp.max(s2, axis=-1, keepdims=True)
            s_ref[hp] = s2 - m2
            m_ref[hp] = jnp.where(is_lo, m2[:BLK], m2[BLK:])

    def pv_from(bufs, v_of, hp):
        s_ref, m_ref = bufs
        num, den = _pair_select(_pair_pv(jnp.exp2(s_ref[hp]), v_of(hp)))
        return num, den, m_ref[hp]

    def group(r4, carry):
        slabs = [r4 + 4 * a for a in range(4)]

        def window16(prev_ref, cur_ref, a):
            return lambda hp: jnp.concatenate(
                [prev_ref[slabs[a], :, cols(hp)], cur_ref[slabs[a], :, cols(hp)]], axis=0)

        def window4(prev_ref, cur_ref, c):
            def rows(hp):
                parts = []
                for s in slabs:
                    if c == 0:
                        parts += [prev_ref[s, BLK - Q4:BLK, cols(hp)], cur_ref[s, 0:Q4, cols(hp)]]
                    else:
                        parts.append(cur_ref[s, (c - 1) * Q4:(c + 1) * Q4, cols(hp)])
                return jnp.concatenate(parts, axis=0)
            return rows

        def quarter_rows(ref, c):
            return lambda hp: jnp.concatenate(
                [ref[s, c * Q4:(c + 1) * Q4, cols(hp)] for s in slabs], axis=0)

        def score16(a, bufs):
            scores_into(bufs, lambda hp: q_ref[slabs[a], :, cols(hp)],
                        window16(kp_ref, kc_ref, a), lambda hp: b16_ref[sel_first, hp])

        def finish16(a, bufs):
            for hp in range(N_PAIRS):
                num, den, top = pv_from(bufs, window16(vp_ref, vc_ref, a), hp)
                n16_ref[a, :, cols(hp)] = num
                d16_ref[a, :, cols(hp)] = den
                m16_ref[a, :, cols(hp)] = top

        def score4(c, bufs):
            sel = sel_first if c == 0 else 1
            scores_into(bufs, quarter_rows(q_ref, c), window4(kp_ref, kc_ref, c),
                        lambda hp: b4_ref[sel, hp])

        def finish4(c, bufs):
            gather = lambda ref, hp: jnp.concatenate(
                [ref[a, c * Q4:(c + 1) * Q4, cols(hp)] for a in range(4)], axis=0)
            lse_tile = jnp.zeros((BLK, PAIR), F32)
            for hp in range(N_PAIRS):
                num4, den4, top4 = pv_from(bufs, window4(vp_ref, vc_ref, c), hp)
                top16 = gather(m16_ref, hp)
                top = jnp.maximum(top16, top4)
                w16 = jnp.exp2(top16 - top)
                w4 = jnp.exp2(top4 - top)
                num = w16 * gather(n16_ref, hp) + w4 * num4
                den = w16 * gather(d16_ref, hp) + w4 * den4
                o = (num / den).astype(BF16)
                lse = LN2 * (top + jnp.log2(den))
                for a, s in enumerate(slabs):
                    op_ref[s, c * Q4:(c + 1) * Q4, cols(hp)] = o[a * Q4:(a + 1) * Q4]
                at = N_ATTN_HEADS * c + 2 * hp
                lse_tile = jnp.where(lane == at, lse, lse_tile)
                lse_tile = jnp.where(lane == at + 1, pltpu.roll(lse, HEAD_DIM, 1), lse_tile)
            for a, s in enumerate(slabs):
                part = lse_tile[a * Q4:(a + 1) * Q4]
                lp_ref[s] = part if c == 0 else lp_ref[s] + part

        stages = ([(functools.partial(score16, a), functools.partial(finish16, a))
                   for a in range(4)]
                  + [(functools.partial(score4, c), functools.partial(finish4, c))
                     for c in range(4)])
        bufs = ((sa_ref, ma_ref), (sb_ref, mb_ref))
        stages[0][0](bufs[0])
        for n, (_, finish) in enumerate(stages):
            if n + 1 < len(stages):
                stages[n + 1][0](bufs[(n + 1) % 2])
            finish(bufs[n % 2])
        return carry

    lax.fori_loop(0, 4, group, 0)

    unperm = unperm_ref[...]
    assert unperm.shape[0] == N_RES * Q4
    l_res = jnp.concatenate([lp_ref[r] for r in range(N_RES)], axis=0)
    hi = l_res.astype(BF16)
    rest = l_res - hi.astype(F32)
    mid = rest.astype(BF16)
    low = (rest - mid.astype(F32)).astype(BF16)
    l_nat = (jnp.dot(unperm, hi, preferred_element_type=F32)
             + jnp.dot(unperm, mid, preferred_element_type=F32)
             + jnp.dot(unperm, low, preferred_element_type=F32))
    for c in range(BLK // Q4):
        rows = slice(c * Q4, (c + 1) * Q4)
        nat = slice(c * Q4 * N_RES, (c + 1) * Q4 * N_RES)
        o_res = jnp.concatenate([op_ref[r, rows, :] for r in range(N_RES)], axis=0)
        o_ref[nat, :] = jnp.dot(unperm, o_res, preferred_element_type=F32).astype(BF16)
        lse_ref[nat, :] = l_nat[:, N_ATTN_HEADS * c:N_ATTN_HEADS * (c + 1)]


def _dil_attn(qr, kr, vr):
    B, T = qr.shape[:2]
    S = T * TILE
    offs = np.arange(2 * BLK)
    b16 = jnp.asarray(_alibi_bias_table(16, BLK + offs[:BLK], offs))
    qo = np.arange(BLK)
    b4 = jnp.asarray(_alibi_bias_table(
        4, BLK + 4 * (qo % Q4) + qo // Q4, 4 * (offs % (2 * Q4)) + offs // (2 * Q4)))
    unperm = jnp.asarray(_residue_permutation(Q4 * N_RES).T, dtype=BF16)
    slab_block = (None, None, N_RES, BLK, D_ATTN)
    cur = lambda b, t: (b, t, 0, 0, 0)
    prev = lambda b, t: (b, jnp.maximum(t - 1, 0), 0, 0, 0)
    nat = lambda b, t: (b, t, 0)
    return pl.pallas_call(
        _dil_attn_kernel,
        grid=(B, T),
        in_specs=[
            pl.BlockSpec(slab_block, cur),
            pl.BlockSpec(slab_block, prev),
            pl.BlockSpec(slab_block, cur),
            pl.BlockSpec(slab_block, prev),
            pl.BlockSpec(slab_block, cur),
            _resident(b16.shape),
            _resident(b4.shape),
            _resident(unperm.shape),
        ],
        out_specs=[
            pl.BlockSpec((None, TILE, D_ATTN), nat),
            pl.BlockSpec((None, TILE, N_ATTN_HEADS), nat),
        ],
        out_shape=[
            jax.ShapeDtypeStruct((B, S, D_ATTN), BF16),
            jax.ShapeDtypeStruct((B, S, N_ATTN_HEADS), F32),
        ],
        scratch_shapes=[pltpu.VMEM((N_PAIRS, 2 * BLK, 2 * BLK), F32)] * 2
        + [pltpu.VMEM((N_PAIRS, BLK, PAIR), F32)] * 2
        + [pltpu.VMEM((4, BLK, D_ATTN), F32)] * 3
        + [pltpu.VMEM((N_RES, BLK, D_ATTN), BF16), pltpu.VMEM((N_RES, Q4, PAIR), F32)],
        compiler_params=pltpu.CompilerParams(
            dimension_semantics=("parallel", "parallel"),
            vmem_limit_bytes=VMEM_LIMIT_BYTES),
    )(qr, kr, kr, vr, vr, b16, b4, unperm)


def _out_ffn_kernel(x_ref, o1_ref, oa_ref, l1_ref, la_ref,
                    yp_ref, ym_ref, expand_ref, wout_ref, gffn_ref, w1_ref, w2_ref,
                    gfin_ref, out_ref, *, sub, ff_chunk):
    def rows_of(rows):
        l1, la = l1_ref[rows, :], la_ref[rows, :]
        top = jnp.maximum(l1, la)
        e1, ea = jnp.exp(l1 - top), jnp.exp(la - top)
        w1 = e1 / (e1 + ea)
        hi = w1.astype(BF16)
        lo = (w1 - hi.astype(F32)).astype(BF16)
        w1 = (jnp.dot(hi, expand_ref[...], preferred_element_type=F32)
              + jnp.dot(lo, expand_ref[...], preferred_element_type=F32))
        oa = oa_ref[rows, :].astype(F32)
        y_attn = oa + w1 * (o1_ref[rows, :].astype(F32) - oa)

        mix = jnp.dot(y_attn.astype(BF16), wout_ref[0:D_ATTN, :], preferred_element_type=F32)
        mix = mix + jnp.dot(yp_ref[rows, :], wout_ref[D_ATTN:D_ATTN + D_POOL, :],
                            preferred_element_type=F32)
        mix = mix + jnp.dot(ym_ref[rows, :], wout_ref[D_ATTN + D_POOL:, :],
                            preferred_element_type=F32)
        x1 = x_ref[rows, :] + mix

        h2 = _rmsnorm(x1, gffn_ref[...]).astype(BF16)
        ff = jnp.zeros(x1.shape, F32)
        for c in range(D_FF // ff_chunk):
            a = jnp.dot(h2, w1_ref[:, c * ff_chunk:(c + 1) * ff_chunk],
                        preferred_element_type=F32)
            a = jnp.maximum(a, 0.0)
            ff = ff + jnp.dot((a * a).astype(BF16), w2_ref[c * ff_chunk:(c + 1) * ff_chunk, :],
                              preferred_element_type=F32)
        out_ref[rows, :] = _rmsnorm(x1 + ff, gfin_ref[...])

    for i in range(x_ref.shape[0] // sub):
        rows_of(slice(i * sub, (i + 1) * sub))


def _out_ffn(x, o_list, lse_list, y_pool, y_mem, w_out, g_ffn, w_ff1, w_ff2, g_final,
             *, tm=1024, sub=512, ff_chunk=1024):
    B, S, _ = x.shape
    expand = jnp.asarray(np.repeat(np.eye(N_ATTN_HEADS, dtype=np.float32), HEAD_DIM, axis=1),
                         dtype=BF16)
    row = lambda b, t: (b, t, 0)
    tile = lambda width: pl.BlockSpec((None, tm, width), row)
    return pl.pallas_call(
        functools.partial(_out_ffn_kernel, sub=sub, ff_chunk=ff_chunk),
        grid=(B, S // tm),
        in_specs=[
            tile(D_MODEL),
            tile(D_ATTN), tile(D_ATTN),
            tile(N_ATTN_HEADS), tile(N_ATTN_HEADS),
            tile(D_POOL), tile(D_XMEM),
            _resident(expand.shape),
            _resident(w_out.shape),
            _resident((1, D_MODEL)),
            _resident(w_ff1.shape),
            _resident(w_ff2.shape),
            _resident((1, D_MODEL)),
        ],
        out_specs=tile(D_MODEL),
        out_shape=jax.ShapeDtypeStruct((B, S, D_MODEL), F32),
        compiler_params=pltpu.CompilerParams(
            dimension_semantics=("parallel", "parallel"),
            vmem_limit_bytes=VMEM_LIMIT_BYTES),
    )(x, *o_list, *lse_list, y_pool, y_mem, expand, w_out, g_ffn, w_ff1, w_ff2, g_final)


def _block_diag(w_pool):
    G, C, E = w_pool.shape
    out = jnp.zeros((G * C, G * E), w_pool.dtype)
    for g in range(G):
        out = out.at[g * C:(g + 1) * C, g * E:(g + 1) * E].set(w_pool[g])
    return out


def kernel(x, mem, g_mix, w_in, g_mem, w_mem_kv, w_pool, pool_scale, w_out,
           g_ffn, w_ff1, w_ff2, g_final):
    depth = w_in.shape[0]
    for i in range(depth):
        km, vm = _memkv(mem, g_mem[i][None], w_mem_kv[i].astype(BF16))
        q, k, v, qr, kr, vr, qm, y_pool = _inproj(
            x, g_mix[i][None], w_in[i].astype(BF16),
            _block_diag(w_pool[i]).astype(BF16), pool_scale[i][None])

        o1, lse1, y_mem = _band_attn(q, k, v, qm, km, vm)
        oa, lsea = _dil_attn(qr, kr, vr)

        assert depth == 1
        x = _out_ffn(x, [o1, oa], [lse1, lsea], y_pool, y_mem, w_out[i].astype(BF16),
                     g_ffn[i][None], w_ff1[i].astype(BF16), w_ff2[i].astype(BF16),
                     g_final[None])
    return x
```

```python
import functools

import numpy as np
import jax
import jax.numpy as jnp
from jax import lax
from jax.experimental import pallas as pl
from jax.experimental.pallas import tpu as pltpu

D_MODEL = 1024
HEAD_DIM = 64
N_ATTN_HEADS = 8
N_MEM_HEADS = 4
POOL_WINDOWS = (2, 4, 8, 16)
POOL_GROUP_DIM = 64
D_ATTN = N_ATTN_HEADS * HEAD_DIM
D_POOL = len(POOL_WINDOWS) * POOL_GROUP_DIM
D_XMEM = N_MEM_HEADS * HEAD_DIM
DILATED = ((128, 1), (512, 4), (2048, 16))
BLK = 128
SPAN = 128
D_FF = 4 * D_MODEL
EPS = 1e-6
POOL_HALO = 16

N_RES = 16
TILE = N_RES * BLK
PAIR = 2 * HEAD_DIM
LANES = 128
N_PAIRS = N_ATTN_HEADS // 2
N_MEM_PAIRS = N_MEM_HEADS // 2

BF16 = jnp.bfloat16
F32 = jnp.float32
LOG2E = 1.4426950408889634
LN2 = 0.6931471805599453

VMEM_LIMIT_BYTES = 56 * 1024 * 1024

assert all(w // d == SPAN for w, d in DILATED)
assert [d for _, d in DILATED] == [1, 4, N_RES]
assert max(POOL_WINDOWS) <= POOL_HALO
assert all(w & (w - 1) == 0 for w in POOL_WINDOWS) and list(POOL_WINDOWS) == sorted(POOL_WINDOWS)


def _rmsnorm(x, g):
    ms = jnp.mean(x * x, axis=-1, keepdims=True)
    return (x * lax.rsqrt(ms + EPS)) * g


def _resident(shape):
    return pl.BlockSpec(shape, lambda *_: (0,) * len(shape), pipeline_mode=pl.Buffered(1))


def _residue_permutation(rows):
    per = rows // N_RES
    p = np.zeros((rows, rows), np.float32)
    for r in range(N_RES):
        for i in range(per):
            p[r * per + i, N_RES * i + r] = 1.0
    return p


def _memkv_kernel(mem_ref, g_ref, w_ref, km_ref, vm_ref):
    mn = _rmsnorm(mem_ref[...], g_ref[...]).astype(BF16)
    kv = jnp.dot(mn, w_ref[...], preferred_element_type=F32)
    km_ref[...] = kv[:, :D_XMEM].astype(BF16)
    vm_ref[...] = kv[:, D_XMEM:].astype(BF16)


def _memkv(mem, g_mem, w_mem_kv):
    B, M, _ = mem.shape
    return pl.pallas_call(
        _memkv_kernel,
        grid=(B,),
        in_specs=[
            pl.BlockSpec((None, M, D_MODEL), lambda b: (b, 0, 0)),
            _resident((1, D_MODEL)),
            _resident((D_MODEL, 2 * D_XMEM)),
        ],
        out_specs=[
            pl.BlockSpec((None, M, D_XMEM), lambda b: (b, 0, 0)),
            pl.BlockSpec((None, M, D_XMEM), lambda b: (b, 0, 0)),
        ],
        out_shape=[jax.ShapeDtypeStruct((B, M, D_XMEM), BF16)] * 2,
    )(mem, g_mem, w_mem_kv)


def _inproj_kernel(x_ref, g_ref, w_ref, wpool_ref, pscale_ref, perm_ref,
                   q_ref, k_ref, v_ref, qr_ref, kr_ref, vr_ref, qm_ref, yp_ref,
                   carry_ref, h_ref, *, tm, sub):
    t = pl.program_id(1)
    per = sub // N_RES

    @pl.when(t == 0)
    def _():
        carry_ref[...] = jnp.zeros((POOL_HALO, D_POOL), F32)

    def rows_of(i):
        rows = slice(i * sub, (i + 1) * sub)
        h_ref[rows, :] = _rmsnorm(x_ref[rows, :], g_ref[...]).astype(BF16)

        def proj(lo, hi):
            return jnp.dot(h_ref[rows, :], w_ref[:, lo:hi], preferred_element_type=F32)

        u = proj(3 * D_ATTN, 3 * D_ATTN + D_POOL)
        run = jnp.concatenate([carry_ref[...], u], axis=0)
        carry_ref[...] = u[sub - POOL_HALO:, :]
        group = lax.broadcasted_iota(jnp.int32, (1, D_POOL), 1) // POOL_GROUP_DIM
        total = None
        width = 1
        for g, w in enumerate(POOL_WINDOWS):
            while width < w:
                run = run + pltpu.roll(run, width, 0)
                width *= 2
            total = run if total is None else jnp.where(group >= g, run, total)
        total = total[POOL_HALO:, :]
        win = jnp.zeros((1, D_POOL), jnp.int32)
        for g, w in enumerate(POOL_WINDOWS):
            win = jnp.where(group == g, w, win)
        pos = t * tm + i * sub + lax.broadcasted_iota(jnp.int32, (sub, 1), 0)
        cnt = jnp.minimum(win, pos + 1).astype(F32)
        d = (total / cnt - u).astype(BF16)

        scale = LOG2E * HEAD_DIM ** -0.5
        q_ref[rows, :] = (proj(0, D_ATTN) * scale).astype(BF16)
        k_ref[rows, :] = proj(D_ATTN, 2 * D_ATTN).astype(BF16)
        v_ref[rows, :] = proj(2 * D_ATTN, 3 * D_ATTN).astype(BF16)
        qm_ref[rows, :] = (proj(3 * D_ATTN + D_POOL, 3 * D_ATTN + D_POOL + D_XMEM)
                           * scale).astype(BF16)

        yp = jnp.dot(d, wpool_ref[...], preferred_element_type=F32) * pscale_ref[...]
        yp_ref[rows, :] = yp.astype(BF16)

        perm = perm_ref[...]
        for nat_ref, res_ref in ((q_ref, qr_ref), (k_ref, kr_ref), (v_ref, vr_ref)):
            moved = jnp.dot(perm, nat_ref[rows, :], preferred_element_type=F32).astype(BF16)
            res_ref[:, i * per:(i + 1) * per, :] = moved.reshape(N_RES, per, D_ATTN)

    for i in range(tm // sub):
        rows_of(i)


def _inproj(x, g_mix, w_in, wpool_bd, pool_scale, *, tm=1024, sub=512):
    B, S, _ = x.shape
    d_in = w_in.shape[1]
    per_tile = TILE // tm
    perm = jnp.asarray(_residue_permutation(sub), dtype=BF16)
    row = lambda b, t: (b, t, 0)
    res = lambda b, t: (b, t // per_tile, 0, t % per_tile, 0)
    nat_spec = pl.BlockSpec((None, tm, D_ATTN), row)
    res_spec = pl.BlockSpec((None, None, N_RES, tm // N_RES, D_ATTN), res)
    nat_shape = jax.ShapeDtypeStruct((B, S, D_ATTN), BF16)
    res_shape = jax.ShapeDtypeStruct((B, S // TILE, N_RES, BLK, D_ATTN), BF16)
    return pl.pallas_call(
        functools.partial(_inproj_kernel, tm=tm, sub=sub),
        grid=(B, S // tm),
        in_specs=[
            pl.BlockSpec((None, tm, D_MODEL), row),
            _resident((1, D_MODEL)),
            _resident((D_MODEL, d_in)),
            _resident((D_POOL, D_POOL)),
            _resident((1, D_POOL)),
            _resident((sub, sub)),
        ],
        out_specs=[
            nat_spec, nat_spec, nat_spec, res_spec, res_spec, res_spec,
            pl.BlockSpec((None, tm, D_XMEM), row),
            pl.BlockSpec((None, tm, D_POOL), row),
        ],
        out_shape=[
            nat_shape, nat_shape, nat_shape, res_shape, res_shape, res_shape,
            jax.ShapeDtypeStruct((B, S, D_XMEM), BF16),
            jax.ShapeDtypeStruct((B, S, D_POOL), BF16),
        ],
        scratch_shapes=[pltpu.VMEM((POOL_HALO, D_POOL), F32),
                        pltpu.VMEM((tm, D_MODEL), BF16)],
        compiler_params=pltpu.CompilerParams(
            dimension_semantics=("arbitrary", "arbitrary"),
            vmem_limit_bytes=VMEM_LIMIT_BYTES),
    )(x, g_mix, w_in, wpool_bd, pool_scale, perm)


def _alibi_bias_table(dilation, q_off, k_off):
    slopes = (2.0 ** (-8.0 * np.arange(1, N_ATTN_HEADS + 1, dtype=np.float32)
                      / N_ATTN_HEADS)).astype(np.float32)
    rel = q_off[:, None] - k_off[None, :]
    valid = (rel >= 0) & (rel <= SPAN)
    bias = -slopes[:, None, None] * (dilation * rel).astype(np.float32)[None] * np.float32(LOG2E)
    full = np.where(valid[None], bias, -np.inf)
    first = np.where((valid & (k_off[None, :] >= BLK))[None], bias, -np.inf)
    table = np.stack([first, full]).astype(np.float32)
    return table.reshape(2, N_PAIRS, 2 * BLK, 2 * BLK)


def _pair_scores(qp, kp):
    is_lo = lax.broadcasted_iota(jnp.int32, (1, PAIR), 1) < HEAD_DIM
    zero = jnp.zeros_like(qp)
    q2 = jnp.concatenate([jnp.where(is_lo, qp, zero), jnp.where(is_lo, zero, qp)], axis=0)
    return lax.dot_general(q2, kp, (((1,), (1,)), ((), ())), preferred_element_type=F32)


def _pair_pv(p2, vp):
    vaug = jnp.concatenate([vp, jnp.ones_like(vp)], axis=1)
    return jnp.dot(p2.astype(BF16), vaug, preferred_element_type=F32)


def _pair_select(out2):
    R = out2.shape[0] // 2
    is_lo = lax.broadcasted_iota(jnp.int32, (1, PAIR), 1) < HEAD_DIM
    num = jnp.where(is_lo, out2[:R, :PAIR], out2[R:, :PAIR])
    den = jnp.where(is_lo, out2[:R, PAIR:], out2[R:, PAIR:])
    return num, den


def _band_attn_kernel(q_ref, kh_ref, kc_ref, vh_ref, vc_ref, bias_ref, qm_ref, km_ref, vm_ref,
                      o_ref, lse_ref, ym_ref,
                      kf_ref, vf_ref, sa_ref, sb_ref, ma_ref, mb_ref, pa_ref, pb_ref, sta_ref, stb_ref,
                      *, tq):
    i = pl.program_id(1)
    kf_ref[0:BLK, :] = kh_ref[...]
    kf_ref[BLK:, :] = kc_ref[...]
    vf_ref[0:BLK, :] = vh_ref[...]
    vf_ref[BLK:, :] = vc_ref[...]
    nsub = tq // BLK

    def scores(j, bufs):
        s_ref, m_ref, _, _ = bufs
        r0 = pl.multiple_of(j * BLK, BLK)
        sel = jnp.where(jnp.logical_and(i == 0, j == 0), 0, 1)
        for hp in range(N_PAIRS):
            cols = slice(hp * PAIR, (hp + 1) * PAIR)
            s2 = _pair_scores(q_ref[pl.ds(r0, BLK), cols], kf_ref[pl.ds(r0, 2 * BLK), cols])
            s2 = s2 + bias_ref[sel, hp]
            s_ref[hp] = s2
            m_ref[hp] = jnp.broadcast_to(jnp.max(s2, axis=-1, keepdims=True), (2 * BLK, BLK))
        for mp in range(N_MEM_PAIRS):
            cols = slice(mp * PAIR, (mp + 1) * PAIR)
            s2 = _pair_scores(qm_ref[pl.ds(r0, BLK), cols], km_ref[:, cols])
            s_ref[N_PAIRS + mp] = s2
            m_ref[N_PAIRS + mp] = jnp.broadcast_to(
                jnp.max(s2, axis=-1, keepdims=True), (2 * BLK, BLK))

    def probs(j, bufs):
        s_ref, m_ref, p_ref, st_ref = bufs
        for u in range(N_PAIRS + N_MEM_PAIRS):
            top = m_ref[u]
            p_ref[u] = jnp.exp2(s_ref[u] - jnp.concatenate([top, top], axis=1)).astype(BF16)
        for h in range(N_ATTN_HEADS):
            rows = slice((h % 2) * BLK, (h % 2 + 1) * BLK)
            st_ref[0, :, h:h + 1] = m_ref[h // 2, rows, h:h + 1]

    def finish(j, bufs):
        _, _, p_ref, st_ref = bufs
        r0 = pl.multiple_of(j * BLK, BLK)
        for hp in range(N_PAIRS):
            cols = slice(hp * PAIR, (hp + 1) * PAIR)
            out2 = _pair_pv(p_ref[hp], vf_ref[pl.ds(r0, 2 * BLK), cols])
            num, den = _pair_select(out2)
            o_ref[pl.ds(r0, BLK), cols] = (num / den).astype(BF16)
            for hh in range(2):
                rows = slice(hh * BLK, (hh + 1) * BLK)
                h = 2 * hp + hh
                st_ref[1, :, h:h + 1] = out2[rows, PAIR + h:PAIR + h + 1]
        lse_ref[pl.ds(r0, BLK), :] = LN2 * (st_ref[0, :, 0:N_ATTN_HEADS]
                                            + jnp.log2(st_ref[1, :, 0:N_ATTN_HEADS]))
        for mp in range(N_MEM_PAIRS):
            cols = slice(mp * PAIR, (mp + 1) * PAIR)
            num, den = _pair_select(_pair_pv(p_ref[N_PAIRS + mp], vm_ref[:, cols]))
            ym_ref[pl.ds(r0, BLK), cols] = (num / den).astype(BF16)

    assert nsub % 2 == 0 and nsub >= 4
    even = (sa_ref, ma_ref, pa_ref, sta_ref)
    odd = (sb_ref, mb_ref, pb_ref, stb_ref)
    scores(0, even)
    scores(1, odd)
    probs(0, even)

    def body(jj, carry):
        n = 2 * jj + 1
        scores(n + 1, even)
        probs(n, odd)
        finish(n - 1, even)
        scores(n + 2, odd)
        probs(n + 1, even)
        finish(n, odd)
        return carry

    lax.fori_loop(0, nsub // 2 - 1, body, 0, unroll=True)
    probs(nsub - 1, odd)
    finish(nsub - 2, even)
    finish(nsub - 1, odd)


def _band_attn(q, k, v, qm, km, vm, *, tq=2048):
    N, L, _ = q.shape
    M = km.shape[1]
    assert M == 2 * BLK
    sub = tq // BLK
    offs = np.arange(2 * BLK)
    bias = jnp.asarray(_alibi_bias_table(1, BLK + offs[:BLK], offs))
    cur = lambda n, i: (n, i, 0)
    halo = lambda n, i: (n, jnp.maximum(i * sub - 1, 0), 0)
    return pl.pallas_call(
        functools.partial(_band_attn_kernel, tq=tq),
        grid=(N, L // tq),
        in_specs=[
            pl.BlockSpec((None, tq, D_ATTN), cur),
            pl.BlockSpec((None, BLK, D_ATTN), halo),
            pl.BlockSpec((None, tq, D_ATTN), cur),
            pl.BlockSpec((None, BLK, D_ATTN), halo),
            pl.BlockSpec((None, tq, D_ATTN), cur),
            _resident(bias.shape),
            pl.BlockSpec((None, tq, D_XMEM), cur),
            pl.BlockSpec((None, M, D_XMEM), lambda n, i: (n, 0, 0)),
            pl.BlockSpec((None, M, D_XMEM), lambda n, i: (n, 0, 0)),
        ],
        out_specs=[
            pl.BlockSpec((None, tq, D_ATTN), cur),
            pl.BlockSpec((None, tq, N_ATTN_HEADS), cur),
            pl.BlockSpec((None, tq, D_XMEM), cur),
        ],
        out_shape=[
            jax.ShapeDtypeStruct((N, L, D_ATTN), BF16),
            jax.ShapeDtypeStruct((N, L, N_ATTN_HEADS), F32),
            jax.ShapeDtypeStruct((N, L, D_XMEM), BF16),
        ],
        scratch_shapes=[pltpu.VMEM((tq + BLK, D_ATTN), BF16)] * 2
        + [pltpu.VMEM((N_PAIRS + N_MEM_PAIRS, 2 * BLK, 2 * BLK), F32)] * 2
        + [pltpu.VMEM((N_PAIRS + N_MEM_PAIRS, 2 * BLK, BLK), F32)] * 2
        + [pltpu.VMEM((N_PAIRS + N_MEM_PAIRS, 2 * BLK, 2 * BLK), BF16)] * 2
        + [pltpu.VMEM((2, BLK, LANES), F32)] * 2,
        compiler_params=pltpu.CompilerParams(
            dimension_semantics=("parallel", "parallel"),
            vmem_limit_bytes=VMEM_LIMIT_BYTES),
    )(q, k, k, v, v, bias, qm, km, vm)


Q4 = BLK // 4


def _dil_attn_kernel(q_ref, kp_ref, kc_ref, vp_ref, vc_ref, b16_ref, b4_ref, unperm_ref,
                     o_ref, lse_ref,
                     sa_ref, sb_ref, ma_ref, mb_ref, pa_ref, pb_ref, ta_ref, tb_ref,
                     n16_ref, d16_ref, m16_ref, op_ref, lp_ref):
    sel_first = jnp.where(pl.program_id(1) == 0, 0, 1)
    is_lo = lax.broadcasted_iota(jnp.int32, (1, PAIR), 1) < HEAD_DIM
    lane = lax.broadcasted_iota(jnp.int32, (1, PAIR), 1)
    cols = lambda hp: slice(hp * PAIR, (hp + 1) * PAIR)

    def scores_into(bufs, q_of, k_of, bias_of):
        s_ref, m_ref, _, _ = bufs
        for hp in range(N_PAIRS):
            s2 = _pair_scores(q_of(hp), k_of(hp)) + bias_of(hp)
            s_ref[hp] = s2
            m_ref[hp] = jnp.broadcast_to(jnp.max(s2, axis=-1, keepdims=True), (2 * BLK, BLK))

    def probs(bufs):
        s_ref, m_ref, p_ref, top_ref = bufs
        for hp in range(N_PAIRS):
            top = m_ref[hp]
            p_ref[hp] = jnp.exp2(s_ref[hp] - jnp.concatenate([top, top], axis=1)).astype(BF16)
            top_ref[hp] = jnp.where(is_lo, top[:BLK], top[BLK:])

    def pv_from(bufs, v_of, hp):
        _, _, p_ref, top_ref = bufs
        num, den = _pair_select(_pair_pv(p_ref[hp], v_of(hp)))
        return num, den, top_ref[hp]

    def group_blocks(r4):
        slabs = [r4 + 4 * a for a in range(4)]
        keep = 0

        def window16(prev_ref, cur_ref, a):
            return lambda hp: jnp.concatenate(
                [prev_ref[slabs[a], :, cols(hp)], cur_ref[slabs[a], :, cols(hp)]], axis=0)

        def window4(prev_ref, cur_ref, c):
            def rows(hp):
                parts = []
                for s in slabs:
                    if c == 0:
                        parts += [prev_ref[s, BLK - Q4:BLK, cols(hp)], cur_ref[s, 0:Q4, cols(hp)]]
                    else:
                        parts.append(cur_ref[s, (c - 1) * Q4:(c + 1) * Q4, cols(hp)])
                return jnp.concatenate(parts, axis=0)
            return rows

        def quarter_rows(ref, c):
            return lambda hp: jnp.concatenate(
                [ref[s, c * Q4:(c + 1) * Q4, cols(hp)] for s in slabs], axis=0)

        def score16(a, bufs):
            scores_into(bufs, lambda hp: q_ref[slabs[a], :, cols(hp)],
                        window16(kp_ref, kc_ref, a), lambda hp: b16_ref[sel_first, hp])

        def finish16(a, bufs):
            for hp in range(N_PAIRS):
                num, den, top = pv_from(bufs, window16(vp_ref, vc_ref, a), hp)
                n16_ref[keep + a, :, cols(hp)] = num
                d16_ref[keep + a, :, cols(hp)] = den
                m16_ref[keep + a, :, cols(hp)] = top

        def score4(c, bufs):
            sel = sel_first if c == 0 else 1
            scores_into(bufs, quarter_rows(q_ref, c), window4(kp_ref, kc_ref, c),
                        lambda hp: b4_ref[sel, hp])

        def finish4(c, bufs):
            gather = lambda ref, hp: jnp.concatenate(
                [ref[keep + a, c * Q4:(c + 1) * Q4, cols(hp)] for a in range(4)], axis=0)
            lse_tile = jnp.zeros((BLK, PAIR), F32)
            for hp in range(N_PAIRS):
                num4, den4, top4 = pv_from(bufs, window4(vp_ref, vc_ref, c), hp)
                top16 = gather(m16_ref, hp)
                top = jnp.maximum(top16, top4)
                w16 = jnp.exp2(top16 - top)
                w4 = jnp.exp2(top4 - top)
                num = w16 * gather(n16_ref, hp) + w4 * num4
                den = w16 * gather(d16_ref, hp) + w4 * den4
                o = (num / den).astype(BF16)
                lse = LN2 * (top + jnp.log2(den))
                for a, s in enumerate(slabs):
                    op_ref[s, c * Q4:(c + 1) * Q4, cols(hp)] = o[a * Q4:(a + 1) * Q4]
                at = N_ATTN_HEADS * c + 2 * hp
                lse_tile = jnp.where(lane == at, lse, lse_tile)
                lse_tile = jnp.where(lane == at + 1, pltpu.roll(lse, HEAD_DIM, 1), lse_tile)
            for a, s in enumerate(slabs):
                part = lse_tile[a * Q4:(a + 1) * Q4]
                lp_ref[s] = part if c == 0 else lp_ref[s] + part

        return ([(functools.partial(score16, a), functools.partial(finish16, a))
                 for a in range(4)]
                + [(functools.partial(score4, c), functools.partial(finish4, c))
                   for c in range(4)])

    bufs = ((sa_ref, ma_ref, pa_ref, ta_ref), (sb_ref, mb_ref, pb_ref, tb_ref))

    def group(r4, carry):
        blocks = group_blocks(r4)
        blocks[0][0](bufs[0])
        for n in range(len(blocks) + 1):
            if n + 1 < len(blocks):
                blocks[n + 1][0](bufs[(n + 1) % 2])
            if n < len(blocks):
                probs(bufs[n % 2])
            if n >= 1:
                blocks[n - 1][1](bufs[(n - 1) % 2])
        return carry

    lax.fori_loop(0, 4, group, 0)

    unperm = unperm_ref[...]
    assert unperm.shape[0] == N_RES * Q4
    l_res = jnp.concatenate([lp_ref[r] for r in range(N_RES)], axis=0)
    hi = l_res.astype(BF16)
    rest = l_res - hi.astype(F32)
    mid = rest.astype(BF16)
    low = (rest - mid.astype(F32)).astype(BF16)
    l_nat = (jnp.dot(unperm, hi, preferred_element_type=F32)
             + jnp.dot(unperm, mid, preferred_element_type=F32)
             + jnp.dot(unperm, low, preferred_element_type=F32))
    for c in range(BLK // Q4):
        rows = slice(c * Q4, (c + 1) * Q4)
        nat = slice(c * Q4 * N_RES, (c + 1) * Q4 * N_RES)
        o_res = jnp.concatenate([op_ref[r, rows, :] for r in range(N_RES)], axis=0)
        o_ref[nat, :] = jnp.dot(unperm, o_res, preferred_element_type=F32).astype(BF16)
        lse_ref[nat, :] = l_nat[:, N_ATTN_HEADS * c:N_ATTN_HEADS * (c + 1)]


def _dil_attn(qr, kr, vr):
    B, T = qr.shape[:2]
    S = T * TILE
    offs = np.arange(2 * BLK)
    b16 = jnp.asarray(_alibi_bias_table(16, BLK + offs[:BLK], offs))
    qo = np.arange(BLK)
    b4 = jnp.asarray(_alibi_bias_table(
        4, BLK + 4 * (qo % Q4) + qo // Q4, 4 * (offs % (2 * Q4)) + offs // (2 * Q4)))
    unperm = jnp.asarray(_residue_permutation(Q4 * N_RES).T, dtype=BF16)
    slab_block = (None, None, N_RES, BLK, D_ATTN)
    cur = lambda b, t: (b, t, 0, 0, 0)
    prev = lambda b, t: (b, jnp.maximum(t - 1, 0), 0, 0, 0)
    nat = lambda b, t: (b, t, 0)
    return pl.pallas_call(
        _dil_attn_kernel,
        grid=(B, T),
        in_specs=[
            pl.BlockSpec(slab_block, cur),
            pl.BlockSpec(slab_block, prev),
            pl.BlockSpec(slab_block, cur),
            pl.BlockSpec(slab_block, prev),
            pl.BlockSpec(slab_block, cur),
            _resident(b16.shape),
            _resident(b4.shape),
            _resident(unperm.shape),
        ],
        out_specs=[
            pl.BlockSpec((None, TILE, D_ATTN), nat),
            pl.BlockSpec((None, TILE, N_ATTN_HEADS), nat),
        ],
        out_shape=[
            jax.ShapeDtypeStruct((B, S, D_ATTN), BF16),
            jax.ShapeDtypeStruct((B, S, N_ATTN_HEADS), F32),
        ],
        scratch_shapes=[pltpu.VMEM((N_PAIRS, 2 * BLK, 2 * BLK), F32)] * 2
        + [pltpu.VMEM((N_PAIRS, 2 * BLK, BLK), F32)] * 2
        + [pltpu.VMEM((N_PAIRS, 2 * BLK, 2 * BLK), BF16)] * 2
        + [pltpu.VMEM((N_PAIRS, BLK, PAIR), F32)] * 2
        + [pltpu.VMEM((4, BLK, D_ATTN), F32)] * 3
        + [pltpu.VMEM((N_RES, BLK, D_ATTN), BF16), pltpu.VMEM((N_RES, Q4, PAIR), F32)],
        compiler_params=pltpu.CompilerParams(
            dimension_semantics=("parallel", "parallel"),
            vmem_limit_bytes=VMEM_LIMIT_BYTES),
    )(qr, kr, kr, vr, vr, b16, b4, unperm)


def _out_ffn_kernel(x_ref, o1_ref, oa_ref, l1_ref, la_ref,
                    yp_ref, ym_ref, expand_ref, wout_ref, gffn_ref, w1_ref, w2_ref,
                    gfin_ref, out_ref, *, sub, ff_chunk):
    def rows_of(rows):
        l1, la = l1_ref[rows, :], la_ref[rows, :]
        top = jnp.maximum(l1, la)
        e1, ea = jnp.exp(l1 - top), jnp.exp(la - top)
        w1 = e1 / (e1 + ea)
        hi = w1.astype(BF16)
        lo = (w1 - hi.astype(F32)).astype(BF16)
        w1 = (jnp.dot(hi, expand_ref[...], preferred_element_type=F32)
              + jnp.dot(lo, expand_ref[...], preferred_element_type=F32))
        oa = oa_ref[rows, :].astype(F32)
        y_attn = oa + w1 * (o1_ref[rows, :].astype(F32) - oa)

        mix = jnp.dot(y_attn.astype(BF16), wout_ref[0:D_ATTN, :], preferred_element_type=F32)
        mix = mix + jnp.dot(yp_ref[rows, :], wout_ref[D_ATTN:D_ATTN + D_POOL, :],
                            preferred_element_type=F32)
        mix = mix + jnp.dot(ym_ref[rows, :], wout_ref[D_ATTN + D_POOL:, :],
                            preferred_element_type=F32)
        x1 = x_ref[rows, :] + mix

        h2 = _rmsnorm(x1, gffn_ref[...]).astype(BF16)
        ff = jnp.zeros(x1.shape, F32)
        for c in range(D_FF // ff_chunk):
            a = jnp.dot(h2, w1_ref[:, c * ff_chunk:(c + 1) * ff_chunk],
                        preferred_element_type=F32)
            a = jnp.maximum(a, 0.0)
            ff = ff + jnp.dot((a * a).astype(BF16), w2_ref[c * ff_chunk:(c + 1) * ff_chunk, :],
                              preferred_element_type=F32)
        out_ref[rows, :] = _rmsnorm(x1 + ff, gfin_ref[...])

    for i in range(x_ref.shape[0] // sub):
        rows_of(slice(i * sub, (i + 1) * sub))


def _out_ffn(x, o_list, lse_list, y_pool, y_mem, w_out, g_ffn, w_ff1, w_ff2, g_final,
             *, tm=1024, sub=512, ff_chunk=1024):
    B, S, _ = x.shape
    expand = jnp.asarray(np.repeat(np.eye(N_ATTN_HEADS, dtype=np.float32), HEAD_DIM, axis=1),
                         dtype=BF16)
    row = lambda b, t: (b, t, 0)
    tile = lambda width: pl.BlockSpec((None, tm, width), row)
    return pl.pallas_call(
        functools.partial(_out_ffn_kernel, sub=sub, ff_chunk=ff_chunk),
        grid=(B, S // tm),
        in_specs=[
            tile(D_MODEL),
            tile(D_ATTN), tile(D_ATTN),
            tile(N_ATTN_HEADS), tile(N_ATTN_HEADS),
            tile(D_POOL), tile(D_XMEM),
            _resident(expand.shape),
            _resident(w_out.shape),
            _resident((1, D_MODEL)),
            _resident(w_ff1.shape),
            _resident(w_ff2.shape),
            _resident((1, D_MODEL)),
        ],
        out_specs=tile(D_MODEL),
        out_shape=jax.ShapeDtypeStruct((B, S, D_MODEL), F32),
        compiler_params=pltpu.CompilerParams(
            dimension_semantics=("parallel", "parallel"),
            vmem_limit_bytes=VMEM_LIMIT_BYTES),
    )(x, *o_list, *lse_list, y_pool, y_mem, expand, w_out, g_ffn, w_ff1, w_ff2, g_final)


def _block_diag(w_pool):
    G, C, E = w_pool.shape
    out = jnp.zeros((G * C, G * E), w_pool.dtype)
    for g in range(G):
        out = out.at[g * C:(g + 1) * C, g * E:(g + 1) * E].set(w_pool[g])
    return out


def kernel(x, mem, g_mix, w_in, g_mem, w_mem_kv, w_pool, pool_scale, w_out,
           g_ffn, w_ff1, w_ff2, g_final):
    depth = w_in.shape[0]
    for i in range(depth):
        km, vm = _memkv(mem, g_mem[i][None], w_mem_kv[i].astype(BF16))
        q, k, v, qr, kr, vr, qm, y_pool = _inproj(
            x, g_mix[i][None], w_in[i].astype(BF16),
            _block_diag(w_pool[i]).astype(BF16), pool_scale[i][None])

        o1, lse1, y_mem = _band_attn(q, k, v, qm, km, vm)
        oa, lsea = _dil_attn(qr, kr, vr)

        assert depth == 1
        x = _out_ffn(x, [o1, oa], [lse1, lsea], y_pool, y_mem, w_out[i].astype(BF16),
                     g_ffn[i][None], w_ff1[i].astype(BF16), w_ff2[i].astype(BF16),
                     g_final[None])
    return x
```

```python
import functools

import numpy as np
import jax
import jax.numpy as jnp
from jax import lax
from jax.experimental import pallas as pl
from jax.experimental.pallas import tpu as pltpu

D_MODEL = 1024
HEAD_DIM = 64
N_ATTN_HEADS = 8
N_MEM_HEADS = 4
POOL_WINDOWS = (2, 4, 8, 16)
POOL_GROUP_DIM = 64
D_ATTN = N_ATTN_HEADS * HEAD_DIM
D_POOL = len(POOL_WINDOWS) * POOL_GROUP_DIM
D_XMEM = N_MEM_HEADS * HEAD_DIM
DILATED = ((128, 1), (512, 4), (2048, 16))
BLK = 128
SPAN = 128
D_FF = 4 * D_MODEL
EPS = 1e-6
POOL_HALO = 16

N_RES = 16
TILE = N_RES * BLK
PAIR = 2 * HEAD_DIM
LANES = 128
PERM_ROWS = 256
PERM_PER = PERM_ROWS // N_RES
N_PAIRS = N_ATTN_HEADS // 2
N_MEM_PAIRS = N_MEM_HEADS // 2

BF16 = jnp.bfloat16
F32 = jnp.float32
LOG2E = 1.4426950408889634
LN2 = 0.6931471805599453

VMEM_LIMIT_BYTES = 56 * 1024 * 1024

assert all(w // d == SPAN for w, d in DILATED)
assert [d for _, d in DILATED] == [1, 4, N_RES]
assert max(POOL_WINDOWS) <= POOL_HALO
assert all(w & (w - 1) == 0 for w in POOL_WINDOWS) and list(POOL_WINDOWS) == sorted(POOL_WINDOWS)


def _rmsnorm(x, g):
    ms = jnp.mean(x * x, axis=-1, keepdims=True)
    return (x * lax.rsqrt(ms + EPS)) * g


def _resident(shape):
    return pl.BlockSpec(shape, lambda *_: (0,) * len(shape), pipeline_mode=pl.Buffered(1))


def _residue_permutation(rows):
    per = rows // N_RES
    p = np.zeros((rows, rows), np.float32)
    for r in range(N_RES):
        for i in range(per):
            p[r * per + i, N_RES * i + r] = 1.0
    return p


def _memkv_kernel(mem_ref, g_ref, w_ref, km_ref, vm_ref):
    mn = _rmsnorm(mem_ref[...], g_ref[...]).astype(BF16)
    kv = jnp.dot(mn, w_ref[...], preferred_element_type=F32)
    km_ref[...] = kv[:, :D_XMEM].astype(BF16)
    vm_ref[...] = kv[:, D_XMEM:].astype(BF16)


def _memkv(mem, g_mem, w_mem_kv):
    B, M, _ = mem.shape
    return pl.pallas_call(
        _memkv_kernel,
        grid=(B,),
        in_specs=[
            pl.BlockSpec((None, M, D_MODEL), lambda b: (b, 0, 0)),
            _resident((1, D_MODEL)),
            _resident((D_MODEL, 2 * D_XMEM)),
        ],
        out_specs=[
            pl.BlockSpec((None, M, D_XMEM), lambda b: (b, 0, 0)),
            pl.BlockSpec((None, M, D_XMEM), lambda b: (b, 0, 0)),
        ],
        out_shape=[jax.ShapeDtypeStruct((B, M, D_XMEM), BF16)] * 2,
    )(mem, g_mem, w_mem_kv)


def _inproj_kernel(x_ref, g_ref, w_ref, wpool_ref, pscale_ref, perm_ref,
                   q_ref, k_ref, v_ref, qr_ref, kr_ref, vr_ref, qm_ref, yp_ref,
                   carry_ref, h_ref, *, tm, sub):
    t = pl.program_id(1)
    per = sub // N_RES

    @pl.when(t == 0)
    def _():
        carry_ref[...] = jnp.zeros((POOL_HALO, D_POOL), F32)

    def rows_of(i):
        rows = slice(i * sub, (i + 1) * sub)
        h_ref[rows, :] = _rmsnorm(x_ref[rows, :], g_ref[...]).astype(BF16)

        def proj(lo, hi):
            return jnp.dot(h_ref[rows, :], w_ref[:, lo:hi], preferred_element_type=F32)

        u = proj(3 * D_ATTN, 3 * D_ATTN + D_POOL)
        run = jnp.concatenate([carry_ref[...], u], axis=0)
        carry_ref[...] = u[sub - POOL_HALO:, :]
        group = lax.broadcasted_iota(jnp.int32, (1, D_POOL), 1) // POOL_GROUP_DIM
        total = None
        width = 1
        for g, w in enumerate(POOL_WINDOWS):
            while width < w:
                run = run + pltpu.roll(run, width, 0)
                width *= 2
            total = run if total is None else jnp.where(group >= g, run, total)
        total = total[POOL_HALO:, :]
        win = jnp.zeros((1, D_POOL), jnp.int32)
        for g, w in enumerate(POOL_WINDOWS):
            win = jnp.where(group == g, w, win)
        pos = t * tm + i * sub + lax.broadcasted_iota(jnp.int32, (sub, 1), 0)
        cnt = jnp.minimum(win, pos + 1).astype(F32)
        d = (total / cnt - u).astype(BF16)

        scale = LOG2E * HEAD_DIM ** -0.5
        q_ref[rows, :] = (proj(0, D_ATTN) * scale).astype(BF16)
        k_ref[rows, :] = proj(D_ATTN, 2 * D_ATTN).astype(BF16)
        v_ref[rows, :] = proj(2 * D_ATTN, 3 * D_ATTN).astype(BF16)
        qm_ref[rows, :] = (proj(3 * D_ATTN + D_POOL, 3 * D_ATTN + D_POOL + D_XMEM)
                           * scale).astype(BF16)

        yp = jnp.dot(d, wpool_ref[...], preferred_element_type=F32) * pscale_ref[...]
        yp_ref[rows, :] = yp.astype(BF16)

        perm = perm_ref[...]
        for nat_ref, res_ref in ((q_ref, qr_ref), (k_ref, kr_ref), (v_ref, vr_ref)):
            for piece in range(sub // PERM_ROWS):
                src = slice(i * sub + piece * PERM_ROWS, i * sub + (piece + 1) * PERM_ROWS)
                dst = slice(i * per + piece * PERM_PER, i * per + (piece + 1) * PERM_PER)
                moved = jnp.dot(perm, nat_ref[src, :], preferred_element_type=F32).astype(BF16)
                res_ref[:, dst, :] = moved.reshape(N_RES, PERM_PER, D_ATTN)

    for i in range(tm // sub):
        rows_of(i)


def _inproj(x, g_mix, w_in, wpool_bd, pool_scale, *, tm=1024, sub=512):
    B, S, _ = x.shape
    d_in = w_in.shape[1]
    per_tile = TILE // tm
    perm = jnp.asarray(_residue_permutation(PERM_ROWS), dtype=BF16)
    row = lambda b, t: (b, t, 0)
    res = lambda b, t: (b, t // per_tile, 0, t % per_tile, 0)
    nat_spec = pl.BlockSpec((None, tm, D_ATTN), row)
    res_spec = pl.BlockSpec((None, None, N_RES, tm // N_RES, D_ATTN), res)
    nat_shape = jax.ShapeDtypeStruct((B, S, D_ATTN), BF16)
    res_shape = jax.ShapeDtypeStruct((B, S // TILE, N_RES, BLK, D_ATTN), BF16)
    return pl.pallas_call(
        functools.partial(_inproj_kernel, tm=tm, sub=sub),
        grid=(B, S // tm),
        in_specs=[
            pl.BlockSpec((None, tm, D_MODEL), row),
            _resident((1, D_MODEL)),
            _resident((D_MODEL, d_in)),
            _resident((D_POOL, D_POOL)),
            _resident((1, D_POOL)),
            _resident((PERM_ROWS, PERM_ROWS)),
        ],
        out_specs=[
            nat_spec, nat_spec, nat_spec, res_spec, res_spec, res_spec,
            pl.BlockSpec((None, tm, D_XMEM), row),
            pl.BlockSpec((None, tm, D_POOL), row),
        ],
        out_shape=[
            nat_shape, nat_shape, nat_shape, res_shape, res_shape, res_shape,
            jax.ShapeDtypeStruct((B, S, D_XMEM), BF16),
            jax.ShapeDtypeStruct((B, S, D_POOL), BF16),
        ],
        scratch_shapes=[pltpu.VMEM((POOL_HALO, D_POOL), F32),
                        pltpu.VMEM((tm, D_MODEL), BF16)],
        compiler_params=pltpu.CompilerParams(
            dimension_semantics=("arbitrary", "arbitrary"),
            vmem_limit_bytes=VMEM_LIMIT_BYTES),
    )(x, g_mix, w_in, wpool_bd, pool_scale, perm)


def _alibi_bias_table(dilation, q_off, k_off):
    slopes = (2.0 ** (-8.0 * np.arange(1, N_ATTN_HEADS + 1, dtype=np.float32)
                      / N_ATTN_HEADS)).astype(np.float32)
    rel = q_off[:, None] - k_off[None, :]
    valid = (rel >= 0) & (rel <= SPAN)
    bias = -slopes[:, None, None] * (dilation * rel).astype(np.float32)[None] * np.float32(LOG2E)
    full = np.where(valid[None], bias, -np.inf)
    first = np.where((valid & (k_off[None, :] >= BLK))[None], bias, -np.inf)
    table = np.stack([first, full]).astype(np.float32)
    return table.reshape(2, N_PAIRS, 2 * BLK, 2 * BLK)


def _pair_scores(qp, kp):
    is_lo = lax.broadcasted_iota(jnp.int32, (1, PAIR), 1) < HEAD_DIM
    zero = jnp.zeros_like(qp)
    q2 = jnp.concatenate([jnp.where(is_lo, qp, zero), jnp.where(is_lo, zero, qp)], axis=0)
    return lax.dot_general(q2, kp, (((1,), (1,)), ((), ())), preferred_element_type=F32)


def _pair_pv(p2, vp):
    vaug = jnp.concatenate([vp, jnp.ones_like(vp)], axis=1)
    return jnp.dot(p2.astype(BF16), vaug, preferred_element_type=F32)


def _pair_select(out2):
    R = out2.shape[0] // 2
    is_lo = lax.broadcasted_iota(jnp.int32, (1, PAIR), 1) < HEAD_DIM
    num = jnp.where(is_lo, out2[:R, :PAIR], out2[R:, :PAIR])
    den = jnp.where(is_lo, out2[:R, PAIR:], out2[R:, PAIR:])
    return num, den


def _band_attn_kernel(q_ref, kh_ref, kc_ref, vh_ref, vc_ref, bias_ref, qm_ref, km_ref, vm_ref,
                      o_ref, lse_ref, ym_ref,
                      kf_ref, vf_ref, sa_ref, sb_ref, ma_ref, mb_ref, pa_ref, pb_ref, sta_ref, stb_ref,
                      *, tq):
    i = pl.program_id(1)
    kf_ref[0:BLK, :] = kh_ref[...]
    kf_ref[BLK:, :] = kc_ref[...]
    vf_ref[0:BLK, :] = vh_ref[...]
    vf_ref[BLK:, :] = vc_ref[...]
    nsub = tq // BLK

    def scores(j, bufs):
        s_ref, m_ref, _, _ = bufs
        r0 = pl.multiple_of(j * BLK, BLK)
        sel = jnp.where(jnp.logical_and(i == 0, j == 0), 0, 1)
        for hp in range(N_PAIRS):
            cols = slice(hp * PAIR, (hp + 1) * PAIR)
            s2 = _pair_scores(q_ref[pl.ds(r0, BLK), cols], kf_ref[pl.ds(r0, 2 * BLK), cols])
            s2 = s2 + bias_ref[sel, hp]
            s_ref[hp] = s2
            m_ref[hp] = jnp.broadcast_to(jnp.max(s2, axis=-1, keepdims=True), (2 * BLK, BLK))
        for mp in range(N_MEM_PAIRS):
            cols = slice(mp * PAIR, (mp + 1) * PAIR)
            s2 = _pair_scores(qm_ref[pl.ds(r0, BLK), cols], km_ref[:, cols])
            s_ref[N_PAIRS + mp] = s2
            m_ref[N_PAIRS + mp] = jnp.broadcast_to(
                jnp.max(s2, axis=-1, keepdims=True), (2 * BLK, BLK))

    def probs(j, bufs):
        s_ref, m_ref, p_ref, st_ref = bufs
        for u in range(N_PAIRS + N_MEM_PAIRS):
            top = m_ref[u]
            p_ref[u] = jnp.exp2(s_ref[u] - jnp.concatenate([top, top], axis=1)).astype(BF16)
        for h in range(N_ATTN_HEADS):
            rows = slice((h % 2) * BLK, (h % 2 + 1) * BLK)
            st_ref[0, :, h:h + 1] = m_ref[h // 2, rows, h:h + 1]

    def finish(j, bufs):
        _, _, p_ref, st_ref = bufs
        r0 = pl.multiple_of(j * BLK, BLK)
        for hp in range(N_PAIRS):
            cols = slice(hp * PAIR, (hp + 1) * PAIR)
            out2 = _pair_pv(p_ref[hp], vf_ref[pl.ds(r0, 2 * BLK), cols])
            num, den = _pair_select(out2)
            o_ref[pl.ds(r0, BLK), cols] = (num / den).astype(BF16)
            for hh in range(2):
                rows = slice(hh * BLK, (hh + 1) * BLK)
                h = 2 * hp + hh
                st_ref[1, :, h:h + 1] = out2[rows, PAIR + h:PAIR + h + 1]
        lse_ref[pl.ds(r0, BLK), :] = LN2 * (st_ref[0, :, 0:N_ATTN_HEADS]
                                            + jnp.log2(st_ref[1, :, 0:N_ATTN_HEADS]))
        for mp in range(N_MEM_PAIRS):
            cols = slice(mp * PAIR, (mp + 1) * PAIR)
            num, den = _pair_select(_pair_pv(p_ref[N_PAIRS + mp], vm_ref[:, cols]))
            ym_ref[pl.ds(r0, BLK), cols] = (num / den).astype(BF16)

    assert nsub % 2 == 0 and nsub >= 4
    even = (sa_ref, ma_ref, pa_ref, sta_ref)
    odd = (sb_ref, mb_ref, pb_ref, stb_ref)
    scores(0, even)
    scores(1, odd)
    probs(0, even)

    def body(jj, carry):
        n = 2 * jj + 1
        scores(n + 1, even)
        probs(n, odd)
        finish(n - 1, even)
        scores(n + 2, odd)
        probs(n + 1, even)
        finish(n, odd)
        return carry

    lax.fori_loop(0, nsub // 2 - 1, body, 0, unroll=True)
    probs(nsub - 1, odd)
    finish(nsub - 2, even)
    finish(nsub - 1, odd)


def _band_attn(q, k, v, qm, km, vm, *, tq=2048):
    N, L, _ = q.shape
    M = km.shape[1]
    assert M == 2 * BLK
    sub = tq // BLK
    offs = np.arange(2 * BLK)
    bias = jnp.asarray(_alibi_bias_table(1, BLK + offs[:BLK], offs))
    cur = lambda n, i: (n, i, 0)
    halo = lambda n, i: (n, jnp.maximum(i * sub - 1, 0), 0)
    return pl.pallas_call(
        functools.partial(_band_attn_kernel, tq=tq),
        grid=(N, L // tq),
        in_specs=[
            pl.BlockSpec((None, tq, D_ATTN), cur),
            pl.BlockSpec((None, BLK, D_ATTN), halo),
            pl.BlockSpec((None, tq, D_ATTN), cur),
            pl.BlockSpec((None, BLK, D_ATTN), halo),
            pl.BlockSpec((None, tq, D_ATTN), cur),
            _resident(bias.shape),
            pl.BlockSpec((None, tq, D_XMEM), cur),
            pl.BlockSpec((None, M, D_XMEM), lambda n, i: (n, 0, 0)),
            pl.BlockSpec((None, M, D_XMEM), lambda n, i: (n, 0, 0)),
        ],
        out_specs=[
            pl.BlockSpec((None, tq, D_ATTN), cur),
            pl.BlockSpec((None, tq, N_ATTN_HEADS), cur),
            pl.BlockSpec((None, tq, D_XMEM), cur),
        ],
        out_shape=[
            jax.ShapeDtypeStruct((N, L, D_ATTN), BF16),
            jax.ShapeDtypeStruct((N, L, N_ATTN_HEADS), F32),
            jax.ShapeDtypeStruct((N, L, D_XMEM), BF16),
        ],
        scratch_shapes=[pltpu.VMEM((tq + BLK, D_ATTN), BF16)] * 2
        + [pltpu.VMEM((N_PAIRS + N_MEM_PAIRS, 2 * BLK, 2 * BLK), F32)] * 2
        + [pltpu.VMEM((N_PAIRS + N_MEM_PAIRS, 2 * BLK, BLK), F32)] * 2
        + [pltpu.VMEM((N_PAIRS + N_MEM_PAIRS, 2 * BLK, 2 * BLK), BF16)] * 2
        + [pltpu.VMEM((2, BLK, LANES), F32)] * 2,
        compiler_params=pltpu.CompilerParams(
            dimension_semantics=("parallel", "parallel"),
            vmem_limit_bytes=VMEM_LIMIT_BYTES),
    )(q, k, k, v, v, bias, qm, km, vm)


Q4 = BLK // 4


def _dil_attn_kernel(q_ref, kp_ref, kc_ref, vp_ref, vc_ref, b16_ref, b4_ref, unperm_ref,
                     o_ref, lse_ref,
                     sa_ref, sb_ref, ma_ref, mb_ref, pa_ref, pb_ref, ta_ref, tb_ref,
                     n16_ref, d16_ref, m16_ref, op_ref, lp_ref):
    sel_first = jnp.where(pl.program_id(1) == 0, 0, 1)
    is_lo = lax.broadcasted_iota(jnp.int32, (1, PAIR), 1) < HEAD_DIM
    lane = lax.broadcasted_iota(jnp.int32, (1, PAIR), 1)
    cols = lambda hp: slice(hp * PAIR, (hp + 1) * PAIR)

    def scores_into(bufs, q_of, k_of, bias_of):
        s_ref, m_ref, _, _ = bufs
        for hp in range(N_PAIRS):
            s2 = _pair_scores(q_of(hp), k_of(hp)) + bias_of(hp)
            s_ref[hp] = s2
            m_ref[hp] = jnp.broadcast_to(jnp.max(s2, axis=-1, keepdims=True), (2 * BLK, BLK))

    def probs(bufs):
        s_ref, m_ref, p_ref, top_ref = bufs
        for hp in range(N_PAIRS):
            top = m_ref[hp]
            p_ref[hp] = jnp.exp2(s_ref[hp] - jnp.concatenate([top, top], axis=1)).astype(BF16)
            top_ref[hp] = jnp.where(is_lo, top[:BLK], top[BLK:])

    def pv_from(bufs, v_of, hp):
        _, _, p_ref, top_ref = bufs
        num, den = _pair_select(_pair_pv(p_ref[hp], v_of(hp)))
        return num, den, top_ref[hp]

    def group_blocks(r4):
        slabs = [r4 + 4 * a for a in range(4)]
        keep = 0

        def window16(prev_ref, cur_ref, a):
            return lambda hp: jnp.concatenate(
                [prev_ref[slabs[a], :, cols(hp)], cur_ref[slabs[a], :, cols(hp)]], axis=0)

        def window4(prev_ref, cur_ref, c):
            def rows(hp):
                parts = []
                for s in slabs:
                    if c == 0:
                        parts += [prev_ref[s, BLK - Q4:BLK, cols(hp)], cur_ref[s, 0:Q4, cols(hp)]]
                    else:
                        parts.append(cur_ref[s, (c - 1) * Q4:(c + 1) * Q4, cols(hp)])
                return jnp.concatenate(parts, axis=0)
            return rows

        def quarter_rows(ref, c):
            return lambda hp: jnp.concatenate(
                [ref[s, c * Q4:(c + 1) * Q4, cols(hp)] for s in slabs], axis=0)

        def score16(a, bufs):
            scores_into(bufs, lambda hp: q_ref[slabs[a], :, cols(hp)],
                        window16(kp_ref, kc_ref, a), lambda hp: b16_ref[sel_first, hp])

        def finish16(a, bufs):
            for hp in range(N_PAIRS):
                num, den, top = pv_from(bufs, window16(vp_ref, vc_ref, a), hp)
                n16_ref[keep + a, :, cols(hp)] = num
                d16_ref[keep + a, :, cols(hp)] = den
                m16_ref[keep + a, :, cols(hp)] = top

        def score4(c, bufs):
            sel = sel_first if c == 0 else 1
            scores_into(bufs, quarter_rows(q_ref, c), window4(kp_ref, kc_ref, c),
                        lambda hp: b4_ref[sel, hp])

        def finish4(c, bufs):
            gather = lambda ref, hp: jnp.concatenate(
                [ref[keep + a, c * Q4:(c + 1) * Q4, cols(hp)] for a in range(4)], axis=0)
            lse_tile = jnp.zeros((BLK, PAIR), F32)
            for hp in range(N_PAIRS):
                num4, den4, top4 = pv_from(bufs, window4(vp_ref, vc_ref, c), hp)
                top16 = gather(m16_ref, hp)
                top = jnp.maximum(top16, top4)
                w16 = jnp.exp2(top16 - top)
                w4 = jnp.exp2(top4 - top)
                num = w16 * gather(n16_ref, hp) + w4 * num4
                den = w16 * gather(d16_ref, hp) + w4 * den4
                o = (num / den).astype(BF16)
                lse = LN2 * (top + jnp.log2(den))
                for a, s in enumerate(slabs):
                    op_ref[s, c * Q4:(c + 1) * Q4, cols(hp)] = o[a * Q4:(a + 1) * Q4]
                at = N_ATTN_HEADS * c + 2 * hp
                lse_tile = jnp.where(lane == at, lse, lse_tile)
                lse_tile = jnp.where(lane == at + 1, pltpu.roll(lse, HEAD_DIM, 1), lse_tile)
            for a, s in enumerate(slabs):
                part = lse_tile[a * Q4:(a + 1) * Q4]
                lp_ref[s] = part if c == 0 else lp_ref[s] + part

        return ([(functools.partial(score16, a), functools.partial(finish16, a))
                 for a in range(4)]
                + [(functools.partial(score4, c), functools.partial(finish4, c))
                   for c in range(4)])

    bufs = ((sa_ref, ma_ref, pa_ref, ta_ref), (sb_ref, mb_ref, pb_ref, tb_ref))

    def group(r4, carry):
        blocks = group_blocks(r4)
        blocks[0][0](bufs[0])
        for n in range(len(blocks) + 1):
            if n + 1 < len(blocks):
                blocks[n + 1][0](bufs[(n + 1) % 2])
            if n < len(blocks):
                probs(bufs[n % 2])
            if n >= 1:
                blocks[n - 1][1](bufs[(n - 1) % 2])
        return carry

    lax.fori_loop(0, 4, group, 0)

    unperm = unperm_ref[...]
    for n in range(BLK // PERM_PER):
        rows = slice(n * PERM_PER, (n + 1) * PERM_PER)
        o_res = jnp.concatenate([op_ref[r, rows, :] for r in range(N_RES)], axis=0)
        o_ref[n * PERM_ROWS:(n + 1) * PERM_ROWS, :] = jnp.dot(
            unperm, o_res, preferred_element_type=F32).astype(BF16)
    for part in range(Q4 // PERM_PER):
        rows = slice(part * PERM_PER, (part + 1) * PERM_PER)
        l_res = jnp.concatenate([lp_ref[r, rows, :] for r in range(N_RES)], axis=0)
        hi = l_res.astype(BF16)
        rest = l_res - hi.astype(F32)
        mid = rest.astype(BF16)
        low = (rest - mid.astype(F32)).astype(BF16)
        l_nat = (jnp.dot(unperm, hi, preferred_element_type=F32)
                 + jnp.dot(unperm, mid, preferred_element_type=F32)
                 + jnp.dot(unperm, low, preferred_element_type=F32))
        for c in range(BLK // Q4):
            first = (c * Q4 + part * PERM_PER) * N_RES
            lse_ref[first:first + PERM_ROWS, :] = l_nat[:, N_ATTN_HEADS * c:N_ATTN_HEADS * (c + 1)]


def _dil_attn(qr, kr, vr):
    B, T = qr.shape[:2]
    S = T * TILE
    offs = np.arange(2 * BLK)
    b16 = jnp.asarray(_alibi_bias_table(16, BLK + offs[:BLK], offs))
    qo = np.arange(BLK)
    b4 = jnp.asarray(_alibi_bias_table(
        4, BLK + 4 * (qo % Q4) + qo // Q4, 4 * (offs % (2 * Q4)) + offs // (2 * Q4)))
    unperm = jnp.asarray(_residue_permutation(PERM_ROWS).T, dtype=BF16)
    slab_block = (None, None, N_RES, BLK, D_ATTN)
    cur = lambda b, t: (b, t, 0, 0, 0)
    prev = lambda b, t: (b, jnp.maximum(t - 1, 0), 0, 0, 0)
    nat = lambda b, t: (b, t, 0)
    return pl.pallas_call(
        _dil_attn_kernel,
        grid=(B, T),
        in_specs=[
            pl.BlockSpec(slab_block, cur),
            pl.BlockSpec(slab_block, prev),
            pl.BlockSpec(slab_block, cur),
            pl.BlockSpec(slab_block, prev),
            pl.BlockSpec(slab_block, cur),
            _resident(b16.shape),
            _resident(b4.shape),
            _resident(unperm.shape),
        ],
        out_specs=[
            pl.BlockSpec((None, TILE, D_ATTN), nat),
            pl.BlockSpec((None, TILE, N_ATTN_HEADS), nat),
        ],
        out_shape=[
            jax.ShapeDtypeStruct((B, S, D_ATTN), BF16),
            jax.ShapeDtypeStruct((B, S, N_ATTN_HEADS), F32),
        ],
        scratch_shapes=[pltpu.VMEM((N_PAIRS, 2 * BLK, 2 * BLK), F32)] * 2
        + [pltpu.VMEM((N_PAIRS, 2 * BLK, BLK), F32)] * 2
        + [pltpu.VMEM((N_PAIRS, 2 * BLK, 2 * BLK), BF16)] * 2
        + [pltpu.VMEM((N_PAIRS, BLK, PAIR), F32)] * 2
        + [pltpu.VMEM((4, BLK, D_ATTN), F32)] * 3
        + [pltpu.VMEM((N_RES, BLK, D_ATTN), BF16), pltpu.VMEM((N_RES, Q4, PAIR), F32)],
        compiler_params=pltpu.CompilerParams(
            dimension_semantics=("parallel", "parallel"),
            vmem_limit_bytes=VMEM_LIMIT_BYTES),
    )(qr, kr, kr, vr, vr, b16, b4, unperm)


def _out_ffn_kernel(x_ref, o1_ref, oa_ref, l1_ref, la_ref,
                    yp_ref, ym_ref, expand_ref, wout_ref, gffn_ref, w1_ref, w2_ref,
                    gfin_ref, out_ref, *, sub, ff_chunk):
    def rows_of(rows):
        l1, la = l1_ref[rows, :], la_ref[rows, :]
        top = jnp.maximum(l1, la)
        e1, ea = jnp.exp(l1 - top), jnp.exp(la - top)
        w1 = e1 / (e1 + ea)
        hi = w1.astype(BF16)
        lo = (w1 - hi.astype(F32)).astype(BF16)
        w1 = (jnp.dot(hi, expand_ref[...], preferred_element_type=F32)
              + jnp.dot(lo, expand_ref[...], preferred_element_type=F32))
        oa = oa_ref[rows, :].astype(F32)
        y_attn = oa + w1 * (o1_ref[rows, :].astype(F32) - oa)

        mix = jnp.dot(y_attn.astype(BF16), wout_ref[0:D_ATTN, :], preferred_element_type=F32)
        mix = mix + jnp.dot(yp_ref[rows, :], wout_ref[D_ATTN:D_ATTN + D_POOL, :],
                            preferred_element_type=F32)
        mix = mix + jnp.dot(ym_ref[rows, :], wout_ref[D_ATTN + D_POOL:, :],
                            preferred_element_type=F32)
        x1 = x_ref[rows, :] + mix

        h2 = _rmsnorm(x1, gffn_ref[...]).astype(BF16)
        ff = jnp.zeros(x1.shape, F32)
        for c in range(D_FF // ff_chunk):
            a = jnp.dot(h2, w1_ref[:, c * ff_chunk:(c + 1) * ff_chunk],
                        preferred_element_type=F32)
            a = jnp.maximum(a, 0.0)
            ff = ff + jnp.dot((a * a).astype(BF16), w2_ref[c * ff_chunk:(c + 1) * ff_chunk, :],
                              preferred_element_type=F32)
        out_ref[rows, :] = _rmsnorm(x1 + ff, gfin_ref[...])

    for i in range(x_ref.shape[0] // sub):
        rows_of(slice(i * sub, (i + 1) * sub))


def _out_ffn(x, o_list, lse_list, y_pool, y_mem, w_out, g_ffn, w_ff1, w_ff2, g_final,
             *, tm=1024, sub=512, ff_chunk=1024):
    B, S, _ = x.shape
    expand = jnp.asarray(np.repeat(np.eye(N_ATTN_HEADS, dtype=np.float32), HEAD_DIM, axis=1),
                         dtype=BF16)
    row = lambda b, t: (b, t, 0)
    tile = lambda width: pl.BlockSpec((None, tm, width), row)
    return pl.pallas_call(
        functools.partial(_out_ffn_kernel, sub=sub, ff_chunk=ff_chunk),
        grid=(B, S // tm),
        in_specs=[
            tile(D_MODEL),
            tile(D_ATTN), tile(D_ATTN),
            tile(N_ATTN_HEADS), tile(N_ATTN_HEADS),
            tile(D_POOL), tile(D_XMEM),
            _resident(expand.shape),
            _resident(w_out.shape),
            _resident((1, D_MODEL)),
            _resident(w_ff1.shape),
            _resident(w_ff2.shape),
            _resident((1, D_MODEL)),
        ],
        out_specs=tile(D_MODEL),
        out_shape=jax.ShapeDtypeStruct((B, S, D_MODEL), F32),
        compiler_params=pltpu.CompilerParams(
            dimension_semantics=("parallel", "parallel"),
            vmem_limit_bytes=VMEM_LIMIT_BYTES),
    )(x, *o_list, *lse_list, y_pool, y_mem, expand, w_out, g_ffn, w_ff1, w_ff2, g_final)


def _block_diag(w_pool):
    G, C, E = w_pool.shape
    out = jnp.zeros((G * C, G * E), w_pool.dtype)
    for g in range(G):
        out = out.at[g * C:(g + 1) * C, g * E:(g + 1) * E].set(w_pool[g])
    return out


def kernel(x, mem, g_mix, w_in, g_mem, w_mem_kv, w_pool, pool_scale, w_out,
           g_ffn, w_ff1, w_ff2, g_final):
    depth = w_in.shape[0]
    for i in range(depth):
        km, vm = _memkv(mem, g_mem[i][None], w_mem_kv[i].astype(BF16))
        q, k, v, qr, kr, vr, qm, y_pool = _inproj(
            x, g_mix[i][None], w_in[i].astype(BF16),
            _block_diag(w_pool[i]).astype(BF16), pool_scale[i][None])

        o1, lse1, y_mem = _band_attn(q, k, v, qm, km, vm)
        oa, lsea = _dil_attn(qr, kr, vr)

        assert depth == 1
        x = _out_ffn(x, [o1, oa], [lse1, lsea], y_pool, y_mem, w_out[i].astype(BF16),
                     g_ffn[i][None], w_ff1[i].astype(BF16), w_ff2[i].astype(BF16),
                     g_final[None])
    return x
```

```python
import functools

import numpy as np
import jax
import jax.numpy as jnp
from jax import lax
from jax.experimental import pallas as pl
from jax.experimental.pallas import tpu as pltpu

D_MODEL = 1024
HEAD_DIM = 64
N_ATTN_HEADS = 8
N_MEM_HEADS = 4
POOL_WINDOWS = (2, 4, 8, 16)
POOL_GROUP_DIM = 64
D_ATTN = N_ATTN_HEADS * HEAD_DIM
D_POOL = len(POOL_WINDOWS) * POOL_GROUP_DIM
D_XMEM = N_MEM_HEADS * HEAD_DIM
DILATED = ((128, 1), (512, 4), (2048, 16))
BLK = 128
SPAN = 128
D_FF = 4 * D_MODEL
EPS = 1e-6
POOL_HALO = 16

N_RES = 16
TILE = N_RES * BLK
PAIR = 2 * HEAD_DIM
LANES = 128
PERM_ROWS = 256
PERM_PER = PERM_ROWS // N_RES
N_PAIRS = N_ATTN_HEADS // 2
N_MEM_PAIRS = N_MEM_HEADS // 2

BF16 = jnp.bfloat16
F32 = jnp.float32
LOG2E = 1.4426950408889634
LN2 = 0.6931471805599453

VMEM_LIMIT_BYTES = 56 * 1024 * 1024

assert all(w // d == SPAN for w, d in DILATED)
assert [d for _, d in DILATED] == [1, 4, N_RES]
assert max(POOL_WINDOWS) <= POOL_HALO
assert all(w & (w - 1) == 0 for w in POOL_WINDOWS) and list(POOL_WINDOWS) == sorted(POOL_WINDOWS)


def _rmsnorm(x, g):
    ms = jnp.mean(x * x, axis=-1, keepdims=True)
    return (x * lax.rsqrt(ms + EPS)) * g


def _resident(shape):
    return pl.BlockSpec(shape, lambda *_: (0,) * len(shape), pipeline_mode=pl.Buffered(1))


def _residue_permutation(rows):
    per = rows // N_RES
    p = np.zeros((rows, rows), np.float32)
    for r in range(N_RES):
        for i in range(per):
            p[r * per + i, N_RES * i + r] = 1.0
    return p


def _memkv_kernel(mem_ref, g_ref, w_ref, km_ref, vm_ref):
    mn = _rmsnorm(mem_ref[...], g_ref[...]).astype(BF16)
    kv = jnp.dot(mn, w_ref[...], preferred_element_type=F32)
    km_ref[...] = kv[:, :D_XMEM].astype(BF16)
    vm_ref[...] = kv[:, D_XMEM:].astype(BF16)


def _memkv(mem, g_mem, w_mem_kv):
    B, M, _ = mem.shape
    return pl.pallas_call(
        _memkv_kernel,
        grid=(B,),
        in_specs=[
            pl.BlockSpec((None, M, D_MODEL), lambda b: (b, 0, 0)),
            _resident((1, D_MODEL)),
            _resident((D_MODEL, 2 * D_XMEM)),
        ],
        out_specs=[
            pl.BlockSpec((None, M, D_XMEM), lambda b: (b, 0, 0)),
            pl.BlockSpec((None, M, D_XMEM), lambda b: (b, 0, 0)),
        ],
        out_shape=[jax.ShapeDtypeStruct((B, M, D_XMEM), BF16)] * 2,
    )(mem, g_mem, w_mem_kv)


def _inproj_kernel(x_ref, g_ref, w_ref, wpool_ref, pscale_ref, perm_ref,
                   q_ref, k_ref, v_ref, qr_ref, kr_ref, vr_ref, qm_ref, yp_ref,
                   carry_ref, h_ref, *, tm, sub):
    t = pl.program_id(1)
    per = sub // N_RES

    @pl.when(t == 0)
    def _():
        carry_ref[...] = jnp.zeros((POOL_HALO, D_POOL), F32)

    def rows_of(i):
        rows = slice(i * sub, (i + 1) * sub)
        h_ref[rows, :] = _rmsnorm(x_ref[rows, :], g_ref[...]).astype(BF16)

        def proj(lo, hi):
            return jnp.dot(h_ref[rows, :], w_ref[:, lo:hi], preferred_element_type=F32)

        u = proj(3 * D_ATTN, 3 * D_ATTN + D_POOL)
        run = jnp.concatenate([carry_ref[...], u], axis=0)
        carry_ref[...] = u[sub - POOL_HALO:, :]
        group = lax.broadcasted_iota(jnp.int32, (1, D_POOL), 1) // POOL_GROUP_DIM
        total = None
        width = 1
        for g, w in enumerate(POOL_WINDOWS):
            while width < w:
                run = run + pltpu.roll(run, width, 0)
                width *= 2
            total = run if total is None else jnp.where(group >= g, run, total)
        total = total[POOL_HALO:, :]
        win = jnp.zeros((1, D_POOL), jnp.int32)
        for g, w in enumerate(POOL_WINDOWS):
            win = jnp.where(group == g, w, win)
        pos = t * tm + i * sub + lax.broadcasted_iota(jnp.int32, (sub, 1), 0)
        cnt = jnp.minimum(win, pos + 1).astype(F32)
        d = (total / cnt - u).astype(BF16)

        scale = LOG2E * HEAD_DIM ** -0.5
        q_ref[rows, :] = (proj(0, D_ATTN) * scale).astype(BF16)
        k_ref[rows, :] = proj(D_ATTN, 2 * D_ATTN).astype(BF16)
        v_ref[rows, :] = proj(2 * D_ATTN, 3 * D_ATTN).astype(BF16)
        qm_ref[rows, :] = (proj(3 * D_ATTN + D_POOL, 3 * D_ATTN + D_POOL + D_XMEM)
                           * scale).astype(BF16)

        yp = jnp.dot(d, wpool_ref[...], preferred_element_type=F32) * pscale_ref[...]
        yp_ref[rows, :] = yp.astype(BF16)

        perm = perm_ref[...]
        for nat_ref, res_ref in ((q_ref, qr_ref), (k_ref, kr_ref), (v_ref, vr_ref)):
            for piece in range(sub // PERM_ROWS):
                src = slice(i * sub + piece * PERM_ROWS, i * sub + (piece + 1) * PERM_ROWS)
                dst = slice(i * per + piece * PERM_PER, i * per + (piece + 1) * PERM_PER)
                moved = jnp.dot(perm, nat_ref[src, :], preferred_element_type=F32).astype(BF16)
                res_ref[:, dst, :] = moved.reshape(N_RES, PERM_PER, D_ATTN)

    for i in range(tm // sub):
        rows_of(i)


def _inproj(x, g_mix, w_in, wpool_bd, pool_scale, *, tm=1024, sub=512):
    B, S, _ = x.shape
    d_in = w_in.shape[1]
    per_tile = TILE // tm
    perm = jnp.asarray(_residue_permutation(PERM_ROWS), dtype=BF16)
    row = lambda b, t: (b, t, 0)
    res = lambda b, t: (b, t // per_tile, 0, t % per_tile, 0)
    nat_spec = pl.BlockSpec((None, tm, D_ATTN), row)
    res_spec = pl.BlockSpec((None, None, N_RES, tm // N_RES, D_ATTN), res)
    nat_shape = jax.ShapeDtypeStruct((B, S, D_ATTN), BF16)
    res_shape = jax.ShapeDtypeStruct((B, S // TILE, N_RES, BLK, D_ATTN), BF16)
    return pl.pallas_call(
        functools.partial(_inproj_kernel, tm=tm, sub=sub),
        grid=(B, S // tm),
        in_specs=[
            pl.BlockSpec((None, tm, D_MODEL), row),
            _resident((1, D_MODEL)),
            _resident((D_MODEL, d_in)),
            _resident((D_POOL, D_POOL)),
            _resident((1, D_POOL)),
            _resident((PERM_ROWS, PERM_ROWS)),
        ],
        out_specs=[
            nat_spec, nat_spec, nat_spec, res_spec, res_spec, res_spec,
            pl.BlockSpec((None, tm, D_XMEM), row),
            pl.BlockSpec((None, tm, D_POOL), row),
        ],
        out_shape=[
            nat_shape, nat_shape, nat_shape, res_shape, res_shape, res_shape,
            jax.ShapeDtypeStruct((B, S, D_XMEM), BF16),
            jax.ShapeDtypeStruct((B, S, D_POOL), BF16),
        ],
        scratch_shapes=[pltpu.VMEM((POOL_HALO, D_POOL), F32),
                        pltpu.VMEM((tm, D_MODEL), BF16)],
        compiler_params=pltpu.CompilerParams(
            dimension_semantics=("arbitrary", "arbitrary"),
            vmem_limit_bytes=VMEM_LIMIT_BYTES),
    )(x, g_mix, w_in, wpool_bd, pool_scale, perm)


def _alibi_bias_table(dilation, q_off, k_off):
    slopes = (2.0 ** (-8.0 * np.arange(1, N_ATTN_HEADS + 1, dtype=np.float32)
                      / N_ATTN_HEADS)).astype(np.float32)
    rel = q_off[:, None] - k_off[None, :]
    valid = (rel >= 0) & (rel <= SPAN)
    bias = -slopes[:, None, None] * (dilation * rel).astype(np.float32)[None] * np.float32(LOG2E)
    full = np.where(valid[None], bias, -np.inf)
    first = np.where((valid & (k_off[None, :] >= BLK))[None], bias, -np.inf)
    table = np.stack([first, full]).astype(np.float32)
    return table.reshape(2, N_PAIRS, 2 * BLK, 2 * BLK)


def _pair_scores(qp, kp):
    is_lo = lax.broadcasted_iota(jnp.int32, (1, PAIR), 1) < HEAD_DIM
    zero = jnp.zeros_like(qp)
    q2 = jnp.concatenate([jnp.where(is_lo, qp, zero), jnp.where(is_lo, zero, qp)], axis=0)
    return lax.dot_general(q2, kp, (((1,), (1,)), ((), ())), preferred_element_type=F32)


def _pair_pv(p2, vp):
    vaug = jnp.concatenate([vp, jnp.ones_like(vp)], axis=1)
    return jnp.dot(p2.astype(BF16), vaug, preferred_element_type=F32)


def _pair_select(out2):
    R = out2.shape[0] // 2
    is_lo = lax.broadcasted_iota(jnp.int32, (1, PAIR), 1) < HEAD_DIM
    num = jnp.where(is_lo, out2[:R, :PAIR], out2[R:, :PAIR])
    den = jnp.where(is_lo, out2[:R, PAIR:], out2[R:, PAIR:])
    return num, den


def _band_attn_kernel(q_ref, kh_ref, kc_ref, vh_ref, vc_ref, bias_ref, qm_ref, km_ref, vm_ref,
                      o_ref, lse_ref, ym_ref,
                      sa_ref, sb_ref, ma_ref, mb_ref, pa_ref, pb_ref, sta_ref, stb_ref, *, tq):
    first_tile = pl.program_id(1) == 0
    nsub = tq // BLK

    def window(halo_ref, cur_ref, j, cols):
        if j == 0:
            return jnp.concatenate([halo_ref[:, cols], cur_ref[0:BLK, cols]], axis=0)
        return cur_ref[(j - 1) * BLK:(j + 1) * BLK, cols]

    def scores(j, bufs):
        s_ref, m_ref, _, _ = bufs
        r0 = j * BLK
        sel = jnp.where(first_tile, 0, 1) if j == 0 else 1
        for hp in range(N_PAIRS):
            cols = slice(hp * PAIR, (hp + 1) * PAIR)
            s2 = _pair_scores(q_ref[pl.ds(r0, BLK), cols], window(kh_ref, kc_ref, j, cols))
            s2 = s2 + bias_ref[sel, hp]
            s_ref[hp] = s2
            m_ref[hp] = jnp.broadcast_to(jnp.max(s2, axis=-1, keepdims=True), (2 * BLK, BLK))
        for mp in range(N_MEM_PAIRS):
            cols = slice(mp * PAIR, (mp + 1) * PAIR)
            s2 = _pair_scores(qm_ref[pl.ds(r0, BLK), cols], km_ref[:, cols])
            s_ref[N_PAIRS + mp] = s2
            m_ref[N_PAIRS + mp] = jnp.broadcast_to(
                jnp.max(s2, axis=-1, keepdims=True), (2 * BLK, BLK))

    def probs(j, bufs):
        s_ref, m_ref, p_ref, st_ref = bufs
        for u in range(N_PAIRS + N_MEM_PAIRS):
            top = m_ref[u]
            p_ref[u] = jnp.exp2(s_ref[u] - jnp.concatenate([top, top], axis=1)).astype(BF16)
        for h in range(N_ATTN_HEADS):
            rows = slice((h % 2) * BLK, (h % 2 + 1) * BLK)
            st_ref[0, :, h:h + 1] = m_ref[h // 2, rows, h:h + 1]

    def finish(j, bufs):
        _, _, p_ref, st_ref = bufs
        r0 = j * BLK
        for hp in range(N_PAIRS):
            cols = slice(hp * PAIR, (hp + 1) * PAIR)
            out2 = _pair_pv(p_ref[hp], window(vh_ref, vc_ref, j, cols))
            num, den = _pair_select(out2)
            o_ref[pl.ds(r0, BLK), cols] = (num / den).astype(BF16)
            for hh in range(2):
                rows = slice(hh * BLK, (hh + 1) * BLK)
                h = 2 * hp + hh
                st_ref[1, :, h:h + 1] = out2[rows, PAIR + h:PAIR + h + 1]
        lse_ref[pl.ds(r0, BLK), :] = LN2 * (st_ref[0, :, 0:N_ATTN_HEADS]
                                            + jnp.log2(st_ref[1, :, 0:N_ATTN_HEADS]))
        for mp in range(N_MEM_PAIRS):
            cols = slice(mp * PAIR, (mp + 1) * PAIR)
            num, den = _pair_select(_pair_pv(p_ref[N_PAIRS + mp], vm_ref[:, cols]))
            ym_ref[pl.ds(r0, BLK), cols] = (num / den).astype(BF16)

    assert nsub % 2 == 0 and nsub >= 4
    even = (sa_ref, ma_ref, pa_ref, sta_ref)
    odd = (sb_ref, mb_ref, pb_ref, stb_ref)
    scores(0, even)
    scores(1, odd)
    probs(0, even)

    for n in range(1, nsub - 1, 2):
        scores(n + 1, even)
        probs(n, odd)
        finish(n - 1, even)
        scores(n + 2, odd)
        probs(n + 1, even)
        finish(n, odd)
    probs(nsub - 1, odd)
    finish(nsub - 2, even)
    finish(nsub - 1, odd)


def _band_attn(q, k, v, qm, km, vm, *, tq=2048):
    N, L, _ = q.shape
    M = km.shape[1]
    assert M == 2 * BLK
    sub = tq // BLK
    offs = np.arange(2 * BLK)
    bias = jnp.asarray(_alibi_bias_table(1, BLK + offs[:BLK], offs))
    cur = lambda n, i: (n, i, 0)
    halo = lambda n, i: (n, jnp.maximum(i * sub - 1, 0), 0)
    return pl.pallas_call(
        functools.partial(_band_attn_kernel, tq=tq),
        grid=(N, L // tq),
        in_specs=[
            pl.BlockSpec((None, tq, D_ATTN), cur),
            pl.BlockSpec((None, BLK, D_ATTN), halo),
            pl.BlockSpec((None, tq, D_ATTN), cur),
            pl.BlockSpec((None, BLK, D_ATTN), halo),
            pl.BlockSpec((None, tq, D_ATTN), cur),
            _resident(bias.shape),
            pl.BlockSpec((None, tq, D_XMEM), cur),
            pl.BlockSpec((None, M, D_XMEM), lambda n, i: (n, 0, 0)),
            pl.BlockSpec((None, M, D_XMEM), lambda n, i: (n, 0, 0)),
        ],
        out_specs=[
            pl.BlockSpec((None, tq, D_ATTN), cur),
            pl.BlockSpec((None, tq, N_ATTN_HEADS), cur),
            pl.BlockSpec((None, tq, D_XMEM), cur),
        ],
        out_shape=[
            jax.ShapeDtypeStruct((N, L, D_ATTN), BF16),
            jax.ShapeDtypeStruct((N, L, N_ATTN_HEADS), F32),
            jax.ShapeDtypeStruct((N, L, D_XMEM), BF16),
        ],
        scratch_shapes=[pltpu.VMEM((N_PAIRS + N_MEM_PAIRS, 2 * BLK, 2 * BLK), F32)] * 2
        + [pltpu.VMEM((N_PAIRS + N_MEM_PAIRS, 2 * BLK, BLK), F32)] * 2
        + [pltpu.VMEM((N_PAIRS + N_MEM_PAIRS, 2 * BLK, 2 * BLK), BF16)] * 2
        + [pltpu.VMEM((2, BLK, LANES), F32)] * 2,
        compiler_params=pltpu.CompilerParams(
            dimension_semantics=("parallel", "parallel"),
            vmem_limit_bytes=VMEM_LIMIT_BYTES),
    )(q, k, k, v, v, bias, qm, km, vm)


Q4 = BLK // 4


def _dil_attn_kernel(q_ref, kp_ref, kc_ref, vp_ref, vc_ref, b16_ref, b4_ref, unperm_ref,
                     o_ref, lse_ref,
                     sa_ref, sb_ref, ma_ref, mb_ref, pa_ref, pb_ref, ta_ref, tb_ref,
                     n16_ref, d16_ref, m16_ref, op_ref, lp_ref):
    sel_first = jnp.where(pl.program_id(1) == 0, 0, 1)
    is_lo = lax.broadcasted_iota(jnp.int32, (1, PAIR), 1) < HEAD_DIM
    lane = lax.broadcasted_iota(jnp.int32, (1, PAIR), 1)
    cols = lambda hp: slice(hp * PAIR, (hp + 1) * PAIR)

    def scores_into(bufs, q_of, k_of, bias_of):
        s_ref, m_ref, _, _ = bufs
        for hp in range(N_PAIRS):
            s2 = _pair_scores(q_of(hp), k_of(hp)) + bias_of(hp)
            s_ref[hp] = s2
            m_ref[hp] = jnp.broadcast_to(jnp.max(s2, axis=-1, keepdims=True), (2 * BLK, BLK))

    def probs(bufs):
        s_ref, m_ref, p_ref, top_ref = bufs
        for hp in range(N_PAIRS):
            top = m_ref[hp]
            p_ref[hp] = jnp.exp2(s_ref[hp] - jnp.concatenate([top, top], axis=1)).astype(BF16)
            top_ref[hp] = jnp.where(is_lo, top[:BLK], top[BLK:])

    def pv_from(bufs, v_of, hp):
        _, _, p_ref, top_ref = bufs
        num, den = _pair_select(_pair_pv(p_ref[hp], v_of(hp)))
        return num, den, top_ref[hp]

    def group_blocks(r4, keep):
        slabs = [r4 + 4 * a for a in range(4)]

        def window16(prev_ref, cur_ref, a):
            return lambda hp: jnp.concatenate(
                [prev_ref[slabs[a], :, cols(hp)], cur_ref[slabs[a], :, cols(hp)]], axis=0)

        def window4(prev_ref, cur_ref, c):
            def rows(hp):
                parts = []
                for s in slabs:
                    if c == 0:
                        parts += [prev_ref[s, BLK - Q4:BLK, cols(hp)], cur_ref[s, 0:Q4, cols(hp)]]
                    else:
                        parts.append(cur_ref[s, (c - 1) * Q4:(c + 1) * Q4, cols(hp)])
                return jnp.concatenate(parts, axis=0)
            return rows

        def quarter_rows(ref, c):
            return lambda hp: jnp.concatenate(
                [ref[s, c * Q4:(c + 1) * Q4, cols(hp)] for s in slabs], axis=0)

        def score16(a, bufs):
            scores_into(bufs, lambda hp: q_ref[slabs[a], :, cols(hp)],
                        window16(kp_ref, kc_ref, a), lambda hp: b16_ref[sel_first, hp])

        def finish16(a, bufs):
            for hp in range(N_PAIRS):
                num, den, top = pv_from(bufs, window16(vp_ref, vc_ref, a), hp)
                n16_ref[keep + a, :, cols(hp)] = num
                d16_ref[keep + a, :, cols(hp)] = den
                m16_ref[keep + a, :, cols(hp)] = top

        def score4(c, bufs):
            sel = sel_first if c == 0 else 1
            scores_into(bufs, quarter_rows(q_ref, c), window4(kp_ref, kc_ref, c),
                        lambda hp: b4_ref[sel, hp])

        def finish4(c, bufs):
            gather = lambda ref, hp: jnp.concatenate(
                [ref[keep + a, c * Q4:(c + 1) * Q4, cols(hp)] for a in range(4)], axis=0)
            lse_tile = jnp.zeros((BLK, PAIR), F32)
            for hp in range(N_PAIRS):
                num4, den4, top4 = pv_from(bufs, window4(vp_ref, vc_ref, c), hp)
                top16 = gather(m16_ref, hp)
                top = jnp.maximum(top16, top4)
                w16 = jnp.exp2(top16 - top)
                w4 = jnp.exp2(top4 - top)
                num = w16 * gather(n16_ref, hp) + w4 * num4
                den = w16 * gather(d16_ref, hp) + w4 * den4
                o = (num / den).astype(BF16)
                lse = LN2 * (top + jnp.log2(den))
                for a, s in enumerate(slabs):
                    op_ref[s, c * Q4:(c + 1) * Q4, cols(hp)] = o[a * Q4:(a + 1) * Q4]
                at = N_ATTN_HEADS * c + 2 * hp
                lse_tile = jnp.where(lane == at, lse, lse_tile)
                lse_tile = jnp.where(lane == at + 1, pltpu.roll(lse, HEAD_DIM, 1), lse_tile)
            for a, s in enumerate(slabs):
                part = lse_tile[a * Q4:(a + 1) * Q4]
                lp_ref[s] = part if c == 0 else lp_ref[s] + part

        return ([(functools.partial(score16, a), functools.partial(finish16, a))
                 for a in range(4)]
                + [(functools.partial(score4, c), functools.partial(finish4, c))
                   for c in range(4)])

    bufs = ((sa_ref, ma_ref, pa_ref, ta_ref), (sb_ref, mb_ref, pb_ref, tb_ref))

    def group(g, carry):
        blocks = group_blocks(2 * g, 0) + group_blocks(2 * g + 1, 4)
        blocks[0][0](bufs[0])
        for n in range(len(blocks) + 1):
            if n + 1 < len(blocks):
                blocks[n + 1][0](bufs[(n + 1) % 2])
            if n < len(blocks):
                probs(bufs[n % 2])
            if n >= 1:
                blocks[n - 1][1](bufs[(n - 1) % 2])
        return carry

    lax.fori_loop(0, 2, group, 0)

    unperm = unperm_ref[...]
    for n in range(BLK // PERM_PER):
        rows = slice(n * PERM_PER, (n + 1) * PERM_PER)
        o_res = jnp.concatenate([op_ref[r, rows, :] for r in range(N_RES)], axis=0)
        o_ref[n * PERM_ROWS:(n + 1) * PERM_ROWS, :] = jnp.dot(
            unperm, o_res, preferred_element_type=F32).astype(BF16)
    for part in range(Q4 // PERM_PER):
        rows = slice(part * PERM_PER, (part + 1) * PERM_PER)
        l_res = jnp.concatenate([lp_ref[r, rows, :] for r in range(N_RES)], axis=0)
        hi = l_res.astype(BF16)
        rest = l_res - hi.astype(F32)
        mid = rest.astype(BF16)
        low = (rest - mid.astype(F32)).astype(BF16)
        l_nat = (jnp.dot(unperm, hi, preferred_element_type=F32)
                 + jnp.dot(unperm, mid, preferred_element_type=F32)
                 + jnp.dot(unperm, low, preferred_element_type=F32))
        for c in range(BLK // Q4):
            first = (c * Q4 + part * PERM_PER) * N_RES
            lse_ref[first:first + PERM_ROWS, :] = l_nat[:, N_ATTN_HEADS * c:N_ATTN_HEADS * (c + 1)]


def _dil_attn(qr, kr, vr):
    B, T = qr.shape[:2]
    S = T * TILE
    offs = np.arange(2 * BLK)
    b16 = jnp.asarray(_alibi_bias_table(16, BLK + offs[:BLK], offs))
    qo = np.arange(BLK)
    b4 = jnp.asarray(_alibi_bias_table(
        4, BLK + 4 * (qo % Q4) + qo // Q4, 4 * (offs % (2 * Q4)) + offs // (2 * Q4)))
    unperm = jnp.asarray(_residue_permutation(PERM_ROWS).T, dtype=BF16)
    slab_block = (None, None, N_RES, BLK, D_ATTN)
    cur = lambda b, t: (b, t, 0, 0, 0)
    prev = lambda b, t: (b, jnp.maximum(t - 1, 0), 0, 0, 0)
    nat = lambda b, t: (b, t, 0)
    return pl.pallas_call(
        _dil_attn_kernel,
        grid=(B, T),
        in_specs=[
            pl.BlockSpec(slab_block, cur),
            pl.BlockSpec(slab_block, prev),
            pl.BlockSpec(slab_block, cur),
            pl.BlockSpec(slab_block, prev),
            pl.BlockSpec(slab_block, cur),
            _resident(b16.shape),
            _resident(b4.shape),
            _resident(unperm.shape),
        ],
        out_specs=[
            pl.BlockSpec((None, TILE, D_ATTN), nat),
            pl.BlockSpec((None, TILE, N_ATTN_HEADS), nat),
        ],
        out_shape=[
            jax.ShapeDtypeStruct((B, S, D_ATTN), BF16),
            jax.ShapeDtypeStruct((B, S, N_ATTN_HEADS), F32),
        ],
        scratch_shapes=[pltpu.VMEM((N_PAIRS, 2 * BLK, 2 * BLK), F32)] * 2
        + [pltpu.VMEM((N_PAIRS, 2 * BLK, BLK), F32)] * 2
        + [pltpu.VMEM((N_PAIRS, 2 * BLK, 2 * BLK), BF16)] * 2
        + [pltpu.VMEM((N_PAIRS, BLK, PAIR), F32)] * 2
        + [pltpu.VMEM((2 * 4, BLK, D_ATTN), F32)] * 3
        + [pltpu.VMEM((N_RES, BLK, D_ATTN), BF16), pltpu.VMEM((N_RES, Q4, PAIR), F32)],
        compiler_params=pltpu.CompilerParams(
            dimension_semantics=("parallel", "parallel"),
            vmem_limit_bytes=VMEM_LIMIT_BYTES),
    )(qr, kr, kr, vr, vr, b16, b4, unperm)


def _out_ffn_kernel(x_ref, o1_ref, oa_ref, l1_ref, la_ref,
                    yp_ref, ym_ref, wout_ref, gffn_ref, w1_ref, w2_ref,
                    gfin_ref, out_ref, *, sub, ff_chunk):
    def rows_of(rows):
        l1, la = l1_ref[rows, :], la_ref[rows, :]
        top = jnp.maximum(l1, la)
        e1, ea = jnp.exp(l1 - top), jnp.exp(la - top)
        w1 = e1 / (e1 + ea)
        is_lo = lax.broadcasted_iota(jnp.int32, (1, PAIR), 1) < HEAD_DIM
        w1 = jnp.concatenate(
            [jnp.where(is_lo, w1[:, 2 * hp:2 * hp + 1], w1[:, 2 * hp + 1:2 * hp + 2])
             for hp in range(N_PAIRS)], axis=1)
        oa = oa_ref[rows, :].astype(F32)
        y_attn = oa + w1 * (o1_ref[rows, :].astype(F32) - oa)

        mix = jnp.dot(y_attn.astype(BF16), wout_ref[0:D_ATTN, :], preferred_element_type=F32)
        mix = mix + jnp.dot(yp_ref[rows, :], wout_ref[D_ATTN:D_ATTN + D_POOL, :],
                            preferred_element_type=F32)
        mix = mix + jnp.dot(ym_ref[rows, :], wout_ref[D_ATTN + D_POOL:, :],
                            preferred_element_type=F32)
        x1 = x_ref[rows, :] + mix

        h2 = _rmsnorm(x1, gffn_ref[...]).astype(BF16)
        ff = jnp.zeros(x1.shape, F32)
        for c in range(D_FF // ff_chunk):
            a = jnp.dot(h2, w1_ref[:, c * ff_chunk:(c + 1) * ff_chunk],
                        preferred_element_type=F32)
            a = jnp.maximum(a, 0.0)
            ff = ff + jnp.dot((a * a).astype(BF16), w2_ref[c * ff_chunk:(c + 1) * ff_chunk, :],
                              preferred_element_type=F32)
        out_ref[rows, :] = _rmsnorm(x1 + ff, gfin_ref[...])

    for i in range(x_ref.shape[0] // sub):
        rows_of(slice(i * sub, (i + 1) * sub))


def _out_ffn(x, o_list, lse_list, y_pool, y_mem, w_out, g_ffn, w_ff1, w_ff2, g_final,
             *, tm=1024, sub=512, ff_chunk=1024):
    B, S, _ = x.shape
    row = lambda b, t: (b, t, 0)
    tile = lambda width: pl.BlockSpec((None, tm, width), row)
    return pl.pallas_call(
        functools.partial(_out_ffn_kernel, sub=sub, ff_chunk=ff_chunk),
        grid=(B, S // tm),
        in_specs=[
            tile(D_MODEL),
            tile(D_ATTN), tile(D_ATTN),
            tile(N_ATTN_HEADS), tile(N_ATTN_HEADS),
            tile(D_POOL), tile(D_XMEM),
            _resident(w_out.shape),
            _resident((1, D_MODEL)),
            _resident(w_ff1.shape),
            _resident(w_ff2.shape),
            _resident((1, D_MODEL)),
        ],
        out_specs=tile(D_MODEL),
        out_shape=jax.ShapeDtypeStruct((B, S, D_MODEL), F32),
        compiler_params=pltpu.CompilerParams(
            dimension_semantics=("parallel", "parallel"),
            vmem_limit_bytes=VMEM_LIMIT_BYTES),
    )(x, *o_list, *lse_list, y_pool, y_mem, w_out, g_ffn, w_ff1, w_ff2, g_final)


def _block_diag(w_pool):
    G, C, E = w_pool.shape
    out = jnp.zeros((G * C, G * E), w_pool.dtype)
    for g in range(G):
        out = out.at[g * C:(g + 1) * C, g * E:(g + 1) * E].set(w_pool[g])
    return out


def kernel(x, mem, g_mix, w_in, g_mem, w_mem_kv, w_pool, pool_scale, w_out,
           g_ffn, w_ff1, w_ff2, g_final):
    depth = w_in.shape[0]
    for i in range(depth):
        km, vm = _memkv(mem, g_mem[i][None], w_mem_kv[i].astype(BF16))
        q, k, v, qr, kr, vr, qm, y_pool = _inproj(
            x, g_mix[i][None], w_in[i].astype(BF16),
            _block_diag(w_pool[i]).astype(BF16), pool_scale[i][None])

        o1, lse1, y_mem = _band_attn(q, k, v, qm, km, vm)
        oa, lsea = _dil_attn(qr, kr, vr)

        assert depth == 1
        x = _out_ffn(x, [o1, oa], [lse1, lsea], y_pool, y_mem, w_out[i].astype(BF16),
                     g_ffn[i][None], w_ff1[i].astype(BF16), w_ff2[i].astype(BF16),
                     g_final[None])
    return x
```

```python
import functools

import numpy as np
import jax
import jax.numpy as jnp
from jax import lax
from jax.experimental import pallas as pl
from jax.experimental.pallas import tpu as pltpu

D_MODEL = 1024
HEAD_DIM = 64
N_ATTN_HEADS = 8
N_MEM_HEADS = 4
POOL_WINDOWS = (2, 4, 8, 16)
POOL_GROUP_DIM = 64
D_ATTN = N_ATTN_HEADS * HEAD_DIM
D_POOL = len(POOL_WINDOWS) * POOL_GROUP_DIM
D_XMEM = N_MEM_HEADS * HEAD_DIM
DILATED = ((128, 1), (512, 4), (2048, 16))
BLK = 128
SPAN = 128
D_FF = 4 * D_MODEL
EPS = 1e-6
POOL_HALO = 16

N_RES = 16
TILE = N_RES * BLK
PAIR = 2 * HEAD_DIM
LANES = 128
PERM_ROWS = 256
PERM_PER = PERM_ROWS // N_RES
N_PAIRS = N_ATTN_HEADS // 2
N_MEM_PAIRS = N_MEM_HEADS // 2

BF16 = jnp.bfloat16
F32 = jnp.float32
LOG2E = 1.4426950408889634
LN2 = 0.6931471805599453

VMEM_LIMIT_BYTES = 56 * 1024 * 1024

assert all(w // d == SPAN for w, d in DILATED)
assert [d for _, d in DILATED] == [1, 4, N_RES]
assert max(POOL_WINDOWS) <= POOL_HALO
assert all(w & (w - 1) == 0 for w in POOL_WINDOWS) and list(POOL_WINDOWS) == sorted(POOL_WINDOWS)


def _rmsnorm(x, g):
    ms = jnp.mean(x * x, axis=-1, keepdims=True)
    return (x * lax.rsqrt(ms + EPS)) * g


def _resident(shape):
    return pl.BlockSpec(shape, lambda *_: (0,) * len(shape), pipeline_mode=pl.Buffered(1))


def _residue_permutation(rows):
    per = rows // N_RES
    p = np.zeros((rows, rows), np.float32)
    for r in range(N_RES):
        for i in range(per):
            p[r * per + i, N_RES * i + r] = 1.0
    return p


def _memkv_kernel(mem_ref, g_ref, w_ref, km_ref, vm_ref):
    mn = _rmsnorm(mem_ref[...], g_ref[...]).astype(BF16)
    kv = jnp.dot(mn, w_ref[...], preferred_element_type=F32)
    km_ref[...] = kv[:, :D_XMEM].astype(BF16)
    vm_ref[...] = kv[:, D_XMEM:].astype(BF16)


def _memkv(mem, g_mem, w_mem_kv):
    B, M, _ = mem.shape
    return pl.pallas_call(
        _memkv_kernel,
        grid=(B,),
        in_specs=[
            pl.BlockSpec((None, M, D_MODEL), lambda b: (b, 0, 0)),
            _resident((1, D_MODEL)),
            _resident((D_MODEL, 2 * D_XMEM)),
        ],
        out_specs=[
            pl.BlockSpec((None, M, D_XMEM), lambda b: (b, 0, 0)),
            pl.BlockSpec((None, M, D_XMEM), lambda b: (b, 0, 0)),
        ],
        out_shape=[jax.ShapeDtypeStruct((B, M, D_XMEM), BF16)] * 2,
    )(mem, g_mem, w_mem_kv)


def _inproj_kernel(x_ref, g_ref, w_ref, wpool_ref, pscale_ref, perm_ref, la_ref, lb_ref, lc_ref,
                   q_ref, k_ref, v_ref, qr_ref, kr_ref, vr_ref, qm_ref, yp_ref,
                   la16_ref, lb16_ref, lc16_ref, carry_ref, h_ref, *, tm, sub):
    t = pl.program_id(1)
    per = sub // N_RES

    for src_ref, dst_ref in ((la_ref, la16_ref), (lb_ref, lb16_ref), (lc_ref, lc16_ref)):
        dst_ref[...] = src_ref[...].astype(BF16)

    @pl.when(t == 0)
    def _():
        carry_ref[...] = jnp.zeros((POOL_HALO, D_POOL), F32)

    def rows_of(i):
        rows = slice(i * sub, (i + 1) * sub)
        h_ref[rows, :] = _rmsnorm(x_ref[rows, :], g_ref[...]).astype(BF16)

        def proj(lo, hi):
            return jnp.dot(h_ref[rows, :], w_ref[:, lo:hi], preferred_element_type=F32)

        u = proj(3 * D_ATTN, 3 * D_ATTN + D_POOL)
        run = jnp.concatenate([carry_ref[...], u], axis=0)
        carry_ref[...] = u[sub - POOL_HALO:, :]
        group = lax.broadcasted_iota(jnp.int32, (1, D_POOL), 1) // POOL_GROUP_DIM
        total = None
        width = 1
        for g, w in enumerate(POOL_WINDOWS):
            while width < w:
                run = run + pltpu.roll(run, width, 0)
                width *= 2
            total = run if total is None else jnp.where(group >= g, run, total)
        total = total[POOL_HALO:, :]
        win = jnp.zeros((1, D_POOL), jnp.int32)
        for g, w in enumerate(POOL_WINDOWS):
            win = jnp.where(group == g, w, win)
        pos = t * tm + i * sub + lax.broadcasted_iota(jnp.int32, (sub, 1), 0)
        cnt = jnp.minimum(win, pos + 1).astype(F32)
        d = (total / cnt - u).astype(BF16)

        scale = LOG2E * HEAD_DIM ** -0.5
        q_ref[rows, :] = (proj(0, D_ATTN) * scale).astype(BF16)
        k_ref[rows, :] = proj(D_ATTN, 2 * D_ATTN).astype(BF16)
        v_ref[rows, :] = proj(2 * D_ATTN, 3 * D_ATTN).astype(BF16)
        qm_ref[rows, :] = (proj(3 * D_ATTN + D_POOL, 3 * D_ATTN + D_POOL + D_XMEM)
                           * scale).astype(BF16)

        yp = jnp.dot(d, wpool_ref[...], preferred_element_type=F32) * pscale_ref[...]
        yp_ref[rows, :] = yp.astype(BF16)

        perm = perm_ref[...]
        for nat_ref, res_ref in ((q_ref, qr_ref), (k_ref, kr_ref), (v_ref, vr_ref)):
            for piece in range(sub // PERM_ROWS):
                src = slice(i * sub + piece * PERM_ROWS, i * sub + (piece + 1) * PERM_ROWS)
                dst = slice(i * per + piece * PERM_PER, i * per + (piece + 1) * PERM_PER)
                moved = jnp.dot(perm, nat_ref[src, :], preferred_element_type=F32).astype(BF16)
                res_ref[:, dst, :] = moved.reshape(N_RES, PERM_PER, D_ATTN)

    for i in range(tm // sub):
        rows_of(i)


def _inproj(x, g_mix, w_in, wpool_bd, pool_scale, later_weights, *, tm=1024, sub=512):
    B, S, _ = x.shape
    d_in = w_in.shape[1]
    per_tile = TILE // tm
    steps = B * (S // tm)
    chunk = lambda b, t: (b * (S // tm) + t, 0)
    later_specs = [pl.BlockSpec((w.shape[0] // steps, w.shape[1]), chunk) for w in later_weights]
    assert all(w.shape[0] % (steps * 16) == 0 for w in later_weights)
    perm = jnp.asarray(_residue_permutation(PERM_ROWS), dtype=BF16)
    row = lambda b, t: (b, t, 0)
    res = lambda b, t: (b, t // per_tile, 0, t % per_tile, 0)
    nat_spec = pl.BlockSpec((None, tm, D_ATTN), row)
    res_spec = pl.BlockSpec((None, None, N_RES, tm // N_RES, D_ATTN), res)
    nat_shape = jax.ShapeDtypeStruct((B, S, D_ATTN), BF16)
    res_shape = jax.ShapeDtypeStruct((B, S // TILE, N_RES, BLK, D_ATTN), BF16)
    return pl.pallas_call(
        functools.partial(_inproj_kernel, tm=tm, sub=sub),
        grid=(B, S // tm),
        in_specs=[
            pl.BlockSpec((None, tm, D_MODEL), row),
            _resident((1, D_MODEL)),
            _resident((D_MODEL, d_in)),
            _resident((D_POOL, D_POOL)),
            _resident((1, D_POOL)),
            _resident((PERM_ROWS, PERM_ROWS)),
            *later_specs,
        ],
        out_specs=[
            nat_spec, nat_spec, nat_spec, res_spec, res_spec, res_spec,
            pl.BlockSpec((None, tm, D_XMEM), row),
            pl.BlockSpec((None, tm, D_POOL), row),
            *later_specs,
        ],
        out_shape=[
            nat_shape, nat_shape, nat_shape, res_shape, res_shape, res_shape,
            jax.ShapeDtypeStruct((B, S, D_XMEM), BF16),
            jax.ShapeDtypeStruct((B, S, D_POOL), BF16),
            *[jax.ShapeDtypeStruct(w.shape, BF16) for w in later_weights],
        ],
        scratch_shapes=[pltpu.VMEM((POOL_HALO, D_POOL), F32),
                        pltpu.VMEM((tm, D_MODEL), BF16)],
        compiler_params=pltpu.CompilerParams(
            dimension_semantics=("arbitrary", "arbitrary"),
            vmem_limit_bytes=VMEM_LIMIT_BYTES),
    )(x, g_mix, w_in, wpool_bd, pool_scale, perm, *later_weights)


def _alibi_bias_table(dilation, q_off, k_off):
    slopes = (2.0 ** (-8.0 * np.arange(1, N_ATTN_HEADS + 1, dtype=np.float32)
                      / N_ATTN_HEADS)).astype(np.float32)
    rel = q_off[:, None] - k_off[None, :]
    valid = (rel >= 0) & (rel <= SPAN)
    bias = -slopes[:, None, None] * (dilation * rel).astype(np.float32)[None] * np.float32(LOG2E)
    full = np.where(valid[None], bias, -np.inf)
    first = np.where((valid & (k_off[None, :] >= BLK))[None], bias, -np.inf)
    table = np.stack([first, full]).astype(np.float32)
    return table.reshape(2, N_PAIRS, 2 * BLK, 2 * BLK)


def _pair_scores(qp, kp):
    is_lo = lax.broadcasted_iota(jnp.int32, (1, PAIR), 1) < HEAD_DIM
    zero = jnp.zeros_like(qp)
    q2 = jnp.concatenate([jnp.where(is_lo, qp, zero), jnp.where(is_lo, zero, qp)], axis=0)
    return lax.dot_general(q2, kp, (((1,), (1,)), ((), ())), preferred_element_type=F32)


def _pair_pv(p2, vp):
    vaug = jnp.concatenate([vp, jnp.ones_like(vp)], axis=1)
    return jnp.dot(p2.astype(BF16), vaug, preferred_element_type=F32)


def _pair_select(out2):
    R = out2.shape[0] // 2
    is_lo = lax.broadcasted_iota(jnp.int32, (1, PAIR), 1) < HEAD_DIM
    num = jnp.where(is_lo, out2[:R, :PAIR], out2[R:, :PAIR])
    den = jnp.where(is_lo, out2[:R, PAIR:], out2[R:, PAIR:])
    return num, den


def _band_attn_kernel(q_ref, kh_ref, kc_ref, vh_ref, vc_ref, bias_ref, qm_ref, km_ref, vm_ref,
                      o_ref, lse_ref, ym_ref,
                      sa_ref, sb_ref, ma_ref, mb_ref, pa_ref, pb_ref, sta_ref, stb_ref, *, tq):
    first_tile = pl.program_id(1) == 0
    nsub = tq // BLK

    def window(halo_ref, cur_ref, j, cols):
        if j == 0:
            return jnp.concatenate([halo_ref[:, cols], cur_ref[0:BLK, cols]], axis=0)
        return cur_ref[(j - 1) * BLK:(j + 1) * BLK, cols]

    def scores(j, bufs):
        s_ref, m_ref, _, _ = bufs
        r0 = j * BLK
        sel = jnp.where(first_tile, 0, 1) if j == 0 else 1
        for hp in range(N_PAIRS):
            cols = slice(hp * PAIR, (hp + 1) * PAIR)
            s2 = _pair_scores(q_ref[pl.ds(r0, BLK), cols], window(kh_ref, kc_ref, j, cols))
            s2 = s2 + bias_ref[sel, hp]
            s_ref[hp] = s2
            m_ref[hp] = jnp.broadcast_to(jnp.max(s2, axis=-1, keepdims=True), (2 * BLK, BLK))
        for mp in range(N_MEM_PAIRS):
            cols = slice(mp * PAIR, (mp + 1) * PAIR)
            s2 = _pair_scores(qm_ref[pl.ds(r0, BLK), cols], km_ref[:, cols])
            s_ref[N_PAIRS + mp] = s2
            m_ref[N_PAIRS + mp] = jnp.broadcast_to(
                jnp.max(s2, axis=-1, keepdims=True), (2 * BLK, BLK))

    def probs(j, bufs):
        s_ref, m_ref, p_ref, st_ref = bufs
        for u in range(N_PAIRS + N_MEM_PAIRS):
            top = m_ref[u]
            p_ref[u] = jnp.exp2(s_ref[u] - jnp.concatenate([top, top], axis=1)).astype(BF16)
        for h in range(N_ATTN_HEADS):
            rows = slice((h % 2) * BLK, (h % 2 + 1) * BLK)
            st_ref[0, :, h:h + 1] = m_ref[h // 2, rows, h:h + 1]

    def finish(j, bufs):
        _, _, p_ref, st_ref = bufs
        r0 = j * BLK
        for hp in range(N_PAIRS):
            cols = slice(hp * PAIR, (hp + 1) * PAIR)
            out2 = _pair_pv(p_ref[hp], window(vh_ref, vc_ref, j, cols))
            num, den = _pair_select(out2)
            o_ref[pl.ds(r0, BLK), cols] = (num / den).astype(BF16)
            for hh in range(2):
                rows = slice(hh * BLK, (hh + 1) * BLK)
                h = 2 * hp + hh
                st_ref[1, :, h:h + 1] = out2[rows, PAIR + h:PAIR + h + 1]
        lse_ref[pl.ds(r0, BLK), :] = LN2 * (st_ref[0, :, 0:N_ATTN_HEADS]
                                            + jnp.log2(st_ref[1, :, 0:N_ATTN_HEADS]))
        for mp in range(N_MEM_PAIRS):
            cols = slice(mp * PAIR, (mp + 1) * PAIR)
            num, den = _pair_select(_pair_pv(p_ref[N_PAIRS + mp], vm_ref[:, cols]))
            ym_ref[pl.ds(r0, BLK), cols] = (num / den).astype(BF16)

    assert nsub % 2 == 0 and nsub >= 4
    even = (sa_ref, ma_ref, pa_ref, sta_ref)
    odd = (sb_ref, mb_ref, pb_ref, stb_ref)
    scores(0, even)
    scores(1, odd)
    probs(0, even)

    for n in range(1, nsub - 1, 2):
        scores(n + 1, even)
        probs(n, odd)
        finish(n - 1, even)
        scores(n + 2, odd)
        probs(n + 1, even)
        finish(n, odd)
    probs(nsub - 1, odd)
    finish(nsub - 2, even)
    finish(nsub - 1, odd)


def _band_attn(q, k, v, qm, km, vm, *, tq=2048):
    N, L, _ = q.shape
    M = km.shape[1]
    assert M == 2 * BLK
    sub = tq // BLK
    offs = np.arange(2 * BLK)
    bias = jnp.asarray(_alibi_bias_table(1, BLK + offs[:BLK], offs))
    cur = lambda n, i: (n, i, 0)
    halo = lambda n, i: (n, jnp.maximum(i * sub - 1, 0), 0)
    return pl.pallas_call(
        functools.partial(_band_attn_kernel, tq=tq),
        grid=(N, L // tq),
        in_specs=[
            pl.BlockSpec((None, tq, D_ATTN), cur),
            pl.BlockSpec((None, BLK, D_ATTN), halo),
            pl.BlockSpec((None, tq, D_ATTN), cur),
            pl.BlockSpec((None, BLK, D_ATTN), halo),
            pl.BlockSpec((None, tq, D_ATTN), cur),
            _resident(bias.shape),
            pl.BlockSpec((None, tq, D_XMEM), cur),
            pl.BlockSpec((None, M, D_XMEM), lambda n, i: (n, 0, 0)),
            pl.BlockSpec((None, M, D_XMEM), lambda n, i: (n, 0, 0)),
        ],
        out_specs=[
            pl.BlockSpec((None, tq, D_ATTN), cur),
            pl.BlockSpec((None, tq, N_ATTN_HEADS), cur),
            pl.BlockSpec((None, tq, D_XMEM), cur),
        ],
        out_shape=[
            jax.ShapeDtypeStruct((N, L, D_ATTN), BF16),
            jax.ShapeDtypeStruct((N, L, N_ATTN_HEADS), F32),
            jax.ShapeDtypeStruct((N, L, D_XMEM), BF16),
        ],
        scratch_shapes=[pltpu.VMEM((N_PAIRS + N_MEM_PAIRS, 2 * BLK, 2 * BLK), F32)] * 2
        + [pltpu.VMEM((N_PAIRS + N_MEM_PAIRS, 2 * BLK, BLK), F32)] * 2
        + [pltpu.VMEM((N_PAIRS + N_MEM_PAIRS, 2 * BLK, 2 * BLK), BF16)] * 2
        + [pltpu.VMEM((2, BLK, LANES), F32)] * 2,
        compiler_params=pltpu.CompilerParams(
            dimension_semantics=("parallel", "parallel"),
            vmem_limit_bytes=VMEM_LIMIT_BYTES),
    )(q, k, k, v, v, bias, qm, km, vm)


Q4 = BLK // 4


def _dil_attn_kernel(q_ref, kp_ref, kc_ref, vp_ref, vc_ref, b16_ref, b4_ref, unperm_ref,
                     o_ref, lse_ref,
                     sa_ref, sb_ref, ma_ref, mb_ref, pa_ref, pb_ref, ta_ref, tb_ref,
                     n16_ref, d16_ref, m16_ref, op_ref, lp_ref):
    sel_first = jnp.where(pl.program_id(1) == 0, 0, 1)
    is_lo = lax.broadcasted_iota(jnp.int32, (1, PAIR), 1) < HEAD_DIM
    lane = lax.broadcasted_iota(jnp.int32, (1, PAIR), 1)
    cols = lambda hp: slice(hp * PAIR, (hp + 1) * PAIR)

    def scores_into(bufs, q_of, k_of, bias_of):
        s_ref, m_ref, _, _ = bufs
        for hp in range(N_PAIRS):
            s2 = _pair_scores(q_of(hp), k_of(hp)) + bias_of(hp)
            s_ref[hp] = s2
            m_ref[hp] = jnp.broadcast_to(jnp.max(s2, axis=-1, keepdims=True), (2 * BLK, BLK))

    def probs(bufs):
        s_ref, m_ref, p_ref, top_ref = bufs
        for hp in range(N_PAIRS):
            top = m_ref[hp]
            p_ref[hp] = jnp.exp2(s_ref[hp] - jnp.concatenate([top, top], axis=1)).astype(BF16)
            top_ref[hp] = jnp.where(is_lo, top[:BLK], top[BLK:])

    def pv_from(bufs, v_of, hp):
        _, _, p_ref, top_ref = bufs
        num, den = _pair_select(_pair_pv(p_ref[hp], v_of(hp)))
        return num, den, top_ref[hp]

    def group_blocks(r4, keep):
        slabs = [r4 + 4 * a for a in range(4)]

        def window16(prev_ref, cur_ref, a):
            return lambda hp: jnp.concatenate(
                [prev_ref[slabs[a], :, cols(hp)], cur_ref[slabs[a], :, cols(hp)]], axis=0)

        def window4(prev_ref, cur_ref, c):
            def rows(hp):
                parts = []
                for s in slabs:
                    if c == 0:
                        parts += [prev_ref[s, BLK - Q4:BLK, cols(hp)], cur_ref[s, 0:Q4, cols(hp)]]
                    else:
                        parts.append(cur_ref[s, (c - 1) * Q4:(c + 1) * Q4, cols(hp)])
                return jnp.concatenate(parts, axis=0)
            return rows

        def quarter_rows(ref, c):
            return lambda hp: jnp.concatenate(
                [ref[s, c * Q4:(c + 1) * Q4, cols(hp)] for s in slabs], axis=0)

        def score16(a, bufs):
            scores_into(bufs, lambda hp: q_ref[slabs[a], :, cols(hp)],
                        window16(kp_ref, kc_ref, a), lambda hp: b16_ref[sel_first, hp])

        def finish16(a, bufs):
            for hp in range(N_PAIRS):
                num, den, top = pv_from(bufs, window16(vp_ref, vc_ref, a), hp)
                n16_ref[keep + a, :, cols(hp)] = num
                d16_ref[keep + a, :, cols(hp)] = den
                m16_ref[keep + a, :, cols(hp)] = top

        def score4(c, bufs):
            sel = sel_first if c == 0 else 1
            scores_into(bufs, quarter_rows(q_ref, c), window4(kp_ref, kc_ref, c),
                        lambda hp: b4_ref[sel, hp])

        def finish4(c, bufs):
            gather = lambda ref, hp: jnp.concatenate(
                [ref[keep + a, c * Q4:(c + 1) * Q4, cols(hp)] for a in range(4)], axis=0)
            lse_tile = jnp.zeros((BLK, PAIR), F32)
            for hp in range(N_PAIRS):
                num4, den4, top4 = pv_from(bufs, window4(vp_ref, vc_ref, c), hp)
                top16 = gather(m16_ref, hp)
                top = jnp.maximum(top16, top4)
                w16 = jnp.exp2(top16 - top)
                w4 = jnp.exp2(top4 - top)
                num = w16 * gather(n16_ref, hp) + w4 * num4
                den = w16 * gather(d16_ref, hp) + w4 * den4
                o = (num / den).astype(BF16)
                lse = LN2 * (top + jnp.log2(den))
                for a, s in enumerate(slabs):
                    op_ref[s, c * Q4:(c + 1) * Q4, cols(hp)] = o[a * Q4:(a + 1) * Q4]
                at = N_ATTN_HEADS * c + 2 * hp
                lse_tile = jnp.where(lane == at, lse, lse_tile)
                lse_tile = jnp.where(lane == at + 1, pltpu.roll(lse, HEAD_DIM, 1), lse_tile)
            for a, s in enumerate(slabs):
                part = lse_tile[a * Q4:(a + 1) * Q4]
                lp_ref[s] = part if c == 0 else lp_ref[s] + part

        return ([(functools.partial(score16, a), functools.partial(finish16, a))
                 for a in range(4)]
                + [(functools.partial(score4, c), functools.partial(finish4, c))
                   for c in range(4)])

    bufs = ((sa_ref, ma_ref, pa_ref, ta_ref), (sb_ref, mb_ref, pb_ref, tb_ref))

    def group(g, carry):
        blocks = group_blocks(2 * g, 0) + group_blocks(2 * g + 1, 4)
        blocks[0][0](bufs[0])
        for n in range(len(blocks) + 1):
            if n + 1 < len(blocks):
                blocks[n + 1][0](bufs[(n + 1) % 2])
            if n < len(blocks):
                probs(bufs[n % 2])
            if n >= 1:
                blocks[n - 1][1](bufs[(n - 1) % 2])
        return carry

    lax.fori_loop(0, 2, group, 0)

    unperm = unperm_ref[...]
    for n in range(BLK // PERM_PER):
        rows = slice(n * PERM_PER, (n + 1) * PERM_PER)
        o_res = jnp.concatenate([op_ref[r, rows, :] for r in range(N_RES)], axis=0)
        o_ref[n * PERM_ROWS:(n + 1) * PERM_ROWS, :] = jnp.dot(
            unperm, o_res, preferred_element_type=F32).astype(BF16)
    for part in range(Q4 // PERM_PER):
        rows = slice(part * PERM_PER, (part + 1) * PERM_PER)
        l_res = jnp.concatenate([lp_ref[r, rows, :] for r in range(N_RES)], axis=0)
        hi = l_res.astype(BF16)
        rest = l_res - hi.astype(F32)
        mid = rest.astype(BF16)
        low = (rest - mid.astype(F32)).astype(BF16)
        l_nat = (jnp.dot(unperm, hi, preferred_element_type=F32)
                 + jnp.dot(unperm, mid, preferred_element_type=F32)
                 + jnp.dot(unperm, low, preferred_element_type=F32))
        for c in range(BLK // Q4):
            first = (c * Q4 + part * PERM_PER) * N_RES
            lse_ref[first:first + PERM_ROWS, :] = l_nat[:, N_ATTN_HEADS * c:N_ATTN_HEADS * (c + 1)]


def _dil_attn(qr, kr, vr):
    B, T = qr.shape[:2]
    S = T * TILE
    offs = np.arange(2 * BLK)
    b16 = jnp.asarray(_alibi_bias_table(16, BLK + offs[:BLK], offs))
    qo = np.arange(BLK)
    b4 = jnp.asarray(_alibi_bias_table(
        4, BLK + 4 * (qo % Q4) + qo // Q4, 4 * (offs % (2 * Q4)) + offs // (2 * Q4)))
    unperm = jnp.asarray(_residue_permutation(PERM_ROWS).T, dtype=BF16)
    slab_block = (None, None, N_RES, BLK, D_ATTN)
    cur = lambda b, t: (b, t, 0, 0, 0)
    prev = lambda b, t: (b, jnp.maximum(t - 1, 0), 0, 0, 0)
    nat = lambda b, t: (b, t, 0)
    return pl.pallas_call(
        _dil_attn_kernel,
        grid=(B, T),
        in_specs=[
            pl.BlockSpec(slab_block, cur),
            pl.BlockSpec(slab_block, prev),
            pl.BlockSpec(slab_block, cur),
            pl.BlockSpec(slab_block, prev),
            pl.BlockSpec(slab_block, cur),
            _resident(b16.shape),
            _resident(b4.shape),
            _resident(unperm.shape),
        ],
        out_specs=[
            pl.BlockSpec((None, TILE, D_ATTN), nat),
            pl.BlockSpec((None, TILE, N_ATTN_HEADS), nat),
        ],
        out_shape=[
            jax.ShapeDtypeStruct((B, S, D_ATTN), BF16),
            jax.ShapeDtypeStruct((B, S, N_ATTN_HEADS), F32),
        ],
        scratch_shapes=[pltpu.VMEM((N_PAIRS, 2 * BLK, 2 * BLK), F32)] * 2
        + [pltpu.VMEM((N_PAIRS, 2 * BLK, BLK), F32)] * 2
        + [pltpu.VMEM((N_PAIRS, 2 * BLK, 2 * BLK), BF16)] * 2
        + [pltpu.VMEM((N_PAIRS, BLK, PAIR), F32)] * 2
        + [pltpu.VMEM((2 * 4, BLK, D_ATTN), F32)] * 3
        + [pltpu.VMEM((N_RES, BLK, D_ATTN), BF16), pltpu.VMEM((N_RES, Q4, PAIR), F32)],
        compiler_params=pltpu.CompilerParams(
            dimension_semantics=("parallel", "parallel"),
            vmem_limit_bytes=VMEM_LIMIT_BYTES),
    )(qr, kr, kr, vr, vr, b16, b4, unperm)


def _out_ffn_kernel(x_ref, o1_ref, oa_ref, l1_ref, la_ref,
                    yp_ref, ym_ref, wout_ref, gffn_ref, w1_ref, w2_ref,
                    gfin_ref, out_ref, *, sub, ff_chunk):
    def rows_of(rows):
        l1, la = l1_ref[rows, :], la_ref[rows, :]
        top = jnp.maximum(l1, la)
        e1, ea = jnp.exp(l1 - top), jnp.exp(la - top)
        w1 = e1 / (e1 + ea)
        is_lo = lax.broadcasted_iota(jnp.int32, (1, PAIR), 1) < HEAD_DIM
        w1 = jnp.concatenate(
            [jnp.where(is_lo, w1[:, 2 * hp:2 * hp + 1], w1[:, 2 * hp + 1:2 * hp + 2])
             for hp in range(N_PAIRS)], axis=1)
        oa = oa_ref[rows, :].astype(F32)
        y_attn = oa + w1 * (o1_ref[rows, :].astype(F32) - oa)

        mix = jnp.dot(y_attn.astype(BF16), wout_ref[0:D_ATTN, :], preferred_element_type=F32)
        mix = mix + jnp.dot(yp_ref[rows, :], wout_ref[D_ATTN:D_ATTN + D_POOL, :],
                            preferred_element_type=F32)
        mix = mix + jnp.dot(ym_ref[rows, :], wout_ref[D_ATTN + D_POOL:, :],
                            preferred_element_type=F32)
        x1 = x_ref[rows, :] + mix

        h2 = _rmsnorm(x1, gffn_ref[...]).astype(BF16)
        ff = jnp.zeros(x1.shape, F32)
        for c in range(D_FF // ff_chunk):
            a = jnp.dot(h2, w1_ref[:, c * ff_chunk:(c + 1) * ff_chunk],
                        preferred_element_type=F32)
            a = jnp.maximum(a, 0.0)
            ff = ff + jnp.dot((a * a).astype(BF16), w2_ref[c * ff_chunk:(c + 1) * ff_chunk, :],
                              preferred_element_type=F32)
        out_ref[rows, :] = _rmsnorm(x1 + ff, gfin_ref[...])

    for i in range(x_ref.shape[0] // sub):
        rows_of(slice(i * sub, (i + 1) * sub))


def _out_ffn(x, o_list, lse_list, y_pool, y_mem, w_out, g_ffn, w_ff1, w_ff2, g_final,
             *, tm=1024, sub=512, ff_chunk=1024):
    B, S, _ = x.shape
    row = lambda b, t: (b, t, 0)
    tile = lambda width: pl.BlockSpec((None, tm, width), row)
    return pl.pallas_call(
        functools.partial(_out_ffn_kernel, sub=sub, ff_chunk=ff_chunk),
        grid=(B, S // tm),
        in_specs=[
            tile(D_MODEL),
            tile(D_ATTN), tile(D_ATTN),
            tile(N_ATTN_HEADS), tile(N_ATTN_HEADS),
            tile(D_POOL), tile(D_XMEM),
            _resident(w_out.shape),
            _resident((1, D_MODEL)),
            _resident(w_ff1.shape),
            _resident(w_ff2.shape),
            _resident((1, D_MODEL)),
        ],
        out_specs=tile(D_MODEL),
        out_shape=jax.ShapeDtypeStruct((B, S, D_MODEL), F32),
        compiler_params=pltpu.CompilerParams(
            dimension_semantics=("parallel", "parallel"),
            vmem_limit_bytes=VMEM_LIMIT_BYTES),
    )(x, *o_list, *lse_list, y_pool, y_mem, w_out, g_ffn, w_ff1, w_ff2, g_final)


def _block_diag(w_pool):
    G, C, E = w_pool.shape
    out = jnp.zeros((G * C, G * E), w_pool.dtype)
    for g in range(G):
        out = out.at[g * C:(g + 1) * C, g * E:(g + 1) * E].set(w_pool[g])
    return out


def kernel(x, mem, g_mix, w_in, g_mem, w_mem_kv, w_pool, pool_scale, w_out,
           g_ffn, w_ff1, w_ff2, g_final):
    depth = w_in.shape[0]
    for i in range(depth):
        km, vm = _memkv(mem, g_mem[i][None], w_mem_kv[i].astype(BF16))
        q, k, v, qr, kr, vr, qm, y_pool, w_out16, w_ff1_16, w_ff2_16 = _inproj(
            x, g_mix[i][None], w_in[i].astype(BF16),
            _block_diag(w_pool[i]).astype(BF16), pool_scale[i][None],
            (w_out[i], w_ff1[i], w_ff2[i]))

        o1, lse1, y_mem = _band_attn(q, k, v, qm, km, vm)
        oa, lsea = _dil_attn(qr, kr, vr)

        assert depth == 1
        x = _out_ffn(x, [o1, oa], [lse1, lsea], y_pool, y_mem, w_out16,
                     g_ffn[i][None], w_ff1_16, w_ff2_16, g_final[None])
    return x
```

```python
import functools

import numpy as np
import jax
import jax.numpy as jnp
from jax import lax
from jax.experimental import pallas as pl
from jax.experimental.pallas import tpu as pltpu

D_MODEL = 1024
HEAD_DIM = 64
N_ATTN_HEADS = 8
N_MEM_HEADS = 4
POOL_WINDOWS = (2, 4, 8, 16)
POOL_GROUP_DIM = 64
D_ATTN = N_ATTN_HEADS * HEAD_DIM
D_POOL = len(POOL_WINDOWS) * POOL_GROUP_DIM
D_XMEM = N_MEM_HEADS * HEAD_DIM
DILATED = ((128, 1), (512, 4), (2048, 16))
BLK = 128
SPAN = 128
D_FF = 4 * D_MODEL
EPS = 1e-6
POOL_HALO = 16

N_RES = 16
TILE = N_RES * BLK
PAIR = 2 * HEAD_DIM
LANES = 128
PERM_ROWS = 256
PERM_PER = PERM_ROWS // N_RES
N_PAIRS = N_ATTN_HEADS // 2
N_MEM_PAIRS = N_MEM_HEADS // 2

BF16 = jnp.bfloat16
F32 = jnp.float32
LOG2E = 1.4426950408889634
LN2 = 0.6931471805599453

VMEM_LIMIT_BYTES = 56 * 1024 * 1024

assert all(w // d == SPAN for w, d in DILATED)
assert [d for _, d in DILATED] == [1, 4, N_RES]
assert max(POOL_WINDOWS) <= POOL_HALO
assert all(w & (w - 1) == 0 for w in POOL_WINDOWS) and list(POOL_WINDOWS) == sorted(POOL_WINDOWS)


def _rmsnorm(x, g):
    ms = jnp.mean(x * x, axis=-1, keepdims=True)
    return (x * lax.rsqrt(ms + EPS)) * g


def _resident(shape):
    return pl.BlockSpec(shape, lambda *_: (0,) * len(shape), pipeline_mode=pl.Buffered(1))


def _residue_permutation(rows):
    per = rows // N_RES
    p = np.zeros((rows, rows), np.float32)
    for r in range(N_RES):
        for i in range(per):
            p[r * per + i, N_RES * i + r] = 1.0
    return p


def _memkv_kernel(mem_ref, g_ref, w_ref, km_ref, vm_ref):
    mn = _rmsnorm(mem_ref[...], g_ref[...]).astype(BF16)
    kv = jnp.dot(mn, w_ref[...], preferred_element_type=F32)
    km_ref[...] = kv[:, :D_XMEM].astype(BF16)
    vm_ref[...] = kv[:, D_XMEM:].astype(BF16)


def _memkv(mem, g_mem, w_mem_kv):
    B, M, _ = mem.shape
    return pl.pallas_call(
        _memkv_kernel,
        grid=(B,),
        in_specs=[
            pl.BlockSpec((None, M, D_MODEL), lambda b: (b, 0, 0)),
            _resident((1, D_MODEL)),
            _resident((D_MODEL, 2 * D_XMEM)),
        ],
        out_specs=[
            pl.BlockSpec((None, M, D_XMEM), lambda b: (b, 0, 0)),
            pl.BlockSpec((None, M, D_XMEM), lambda b: (b, 0, 0)),
        ],
        out_shape=[jax.ShapeDtypeStruct((B, M, D_XMEM), BF16)] * 2,
    )(mem, g_mem, w_mem_kv)


def _inproj_kernel(x_ref, g_ref, w_ref, wpool_ref, pscale_ref, perm_ref, la_ref, lb_ref, lc_ref,
                   q_ref, k_ref, v_ref, qr_ref, kr_ref, vr_ref, qm_ref, yp_ref,
                   la16_ref, lb16_ref, lc16_ref, carry_ref, h_ref, *, tm, sub):
    t = pl.program_id(1)
    per = sub // N_RES

    for src_ref, dst_ref in ((la_ref, la16_ref), (lb_ref, lb16_ref), (lc_ref, lc16_ref)):
        dst_ref[...] = src_ref[...].astype(BF16)

    @pl.when(t == 0)
    def _():
        carry_ref[...] = jnp.zeros((POOL_HALO, D_POOL), F32)

    def rows_of(i):
        rows = slice(i * sub, (i + 1) * sub)
        h_ref[rows, :] = _rmsnorm(x_ref[rows, :], g_ref[...]).astype(BF16)

        def proj(lo, hi):
            return jnp.dot(h_ref[rows, :], w_ref[:, lo:hi], preferred_element_type=F32)

        u = proj(3 * D_ATTN, 3 * D_ATTN + D_POOL)
        run = jnp.concatenate([carry_ref[...], u], axis=0)
        carry_ref[...] = u[sub - POOL_HALO:, :]
        group = lax.broadcasted_iota(jnp.int32, (1, D_POOL), 1) // POOL_GROUP_DIM
        total = None
        width = 1
        for g, w in enumerate(POOL_WINDOWS):
            while width < w:
                run = run + pltpu.roll(run, width, 0)
                width *= 2
            total = run if total is None else jnp.where(group >= g, run, total)
        total = total[POOL_HALO:, :]
        win = jnp.zeros((1, D_POOL), jnp.int32)
        for g, w in enumerate(POOL_WINDOWS):
            win = jnp.where(group == g, w, win)
        pos = t * tm + i * sub + lax.broadcasted_iota(jnp.int32, (sub, 1), 0)
        cnt = jnp.minimum(win, pos + 1).astype(F32)
        d = (total / cnt - u).astype(BF16)

        scale = LOG2E * HEAD_DIM ** -0.5
        q_ref[rows, :] = (proj(0, D_ATTN) * scale).astype(BF16)
        k_ref[rows, :] = proj(D_ATTN, 2 * D_ATTN).astype(BF16)
        v_ref[rows, :] = proj(2 * D_ATTN, 3 * D_ATTN).astype(BF16)
        qm_ref[rows, :] = (proj(3 * D_ATTN + D_POOL, 3 * D_ATTN + D_POOL + D_XMEM)
                           * scale).astype(BF16)

        yp = jnp.dot(d, wpool_ref[...], preferred_element_type=F32) * pscale_ref[...]
        yp_ref[rows, :] = yp.astype(BF16)

        perm = perm_ref[...]
        for nat_ref, res_ref in ((q_ref, qr_ref), (k_ref, kr_ref), (v_ref, vr_ref)):
            for piece in range(sub // PERM_ROWS):
                src = slice(i * sub + piece * PERM_ROWS, i * sub + (piece + 1) * PERM_ROWS)
                dst = slice(i * per + piece * PERM_PER, i * per + (piece + 1) * PERM_PER)
                moved = jnp.dot(perm, nat_ref[src, :], preferred_element_type=F32).astype(BF16)
                res_ref[:, dst, :] = moved.reshape(N_RES, PERM_PER, D_ATTN)

    for i in range(tm // sub):
        rows_of(i)


def _inproj(x, g_mix, w_in, wpool_bd, pool_scale, later_weights, *, tm=1024, sub=512):
    B, S, _ = x.shape
    d_in = w_in.shape[1]
    per_tile = TILE // tm
    steps = B * (S // tm)
    chunk = lambda b, t: (b * (S // tm) + t, 0)
    later_specs = [pl.BlockSpec((w.shape[0] // steps, w.shape[1]), chunk) for w in later_weights]
    assert all(w.shape[0] % (steps * 16) == 0 for w in later_weights)
    perm = jnp.asarray(_residue_permutation(PERM_ROWS), dtype=BF16)
    row = lambda b, t: (b, t, 0)
    res = lambda b, t: (b, t // per_tile, 0, t % per_tile, 0)
    nat_spec = pl.BlockSpec((None, tm, D_ATTN), row)
    res_spec = pl.BlockSpec((None, None, N_RES, tm // N_RES, D_ATTN), res)
    nat_shape = jax.ShapeDtypeStruct((B, S, D_ATTN), BF16)
    res_shape = jax.ShapeDtypeStruct((B, S // TILE, N_RES, BLK, D_ATTN), BF16)
    return pl.pallas_call(
        functools.partial(_inproj_kernel, tm=tm, sub=sub),
        grid=(B, S // tm),
        in_specs=[
            pl.BlockSpec((None, tm, D_MODEL), row),
            _resident((1, D_MODEL)),
            _resident((D_MODEL, d_in)),
            _resident((D_POOL, D_POOL)),
            _resident((1, D_POOL)),
            _resident((PERM_ROWS, PERM_ROWS)),
            *later_specs,
        ],
        out_specs=[
            nat_spec, nat_spec, nat_spec, res_spec, res_spec, res_spec,
            pl.BlockSpec((None, tm, D_XMEM), row),
            pl.BlockSpec((None, tm, D_POOL), row),
            *later_specs,
        ],
        out_shape=[
            nat_shape, nat_shape, nat_shape, res_shape, res_shape, res_shape,
            jax.ShapeDtypeStruct((B, S, D_XMEM), BF16),
            jax.ShapeDtypeStruct((B, S, D_POOL), BF16),
            *[jax.ShapeDtypeStruct(w.shape, BF16) for w in later_weights],
        ],
        scratch_shapes=[pltpu.VMEM((POOL_HALO, D_POOL), F32),
                        pltpu.VMEM((tm, D_MODEL), BF16)],
        compiler_params=pltpu.CompilerParams(
            dimension_semantics=("arbitrary", "arbitrary"),
            vmem_limit_bytes=VMEM_LIMIT_BYTES),
    )(x, g_mix, w_in, wpool_bd, pool_scale, perm, *later_weights)


def _alibi_bias_table(dilation, q_off, k_off):
    slopes = (2.0 ** (-8.0 * np.arange(1, N_ATTN_HEADS + 1, dtype=np.float32)
                      / N_ATTN_HEADS)).astype(np.float32)
    rel = q_off[:, None] - k_off[None, :]
    valid = (rel >= 0) & (rel <= SPAN)
    bias = -slopes[:, None, None] * (dilation * rel).astype(np.float32)[None] * np.float32(LOG2E)
    full = np.where(valid[None], bias, -np.inf)
    first = np.where((valid & (k_off[None, :] >= BLK))[None], bias, -np.inf)
    table = np.stack([first, full]).astype(np.float32)
    return table.reshape(2, N_PAIRS, 2 * BLK, 2 * BLK)


def _pair_scores(qp, kp):
    is_lo = lax.broadcasted_iota(jnp.int32, (1, PAIR), 1) < HEAD_DIM
    zero = jnp.zeros_like(qp)
    q2 = jnp.concatenate([jnp.where(is_lo, qp, zero), jnp.where(is_lo, zero, qp)], axis=0)
    return lax.dot_general(q2, kp, (((1,), (1,)), ((), ())), preferred_element_type=F32)


def _pair_pv(p2, vp):
    vaug = jnp.concatenate([vp, jnp.ones_like(vp)], axis=1)
    return jnp.dot(p2.astype(BF16), vaug, preferred_element_type=F32)


def _pair_select(out2):
    R = out2.shape[0] // 2
    is_lo = lax.broadcasted_iota(jnp.int32, (1, PAIR), 1) < HEAD_DIM
    num = jnp.where(is_lo, out2[:R, :PAIR], out2[R:, :PAIR])
    den = jnp.where(is_lo, out2[:R, PAIR:], out2[R:, PAIR:])
    return num, den


def _band_attn_kernel(q_ref, kh_ref, kc_ref, vh_ref, vc_ref, bias_ref, qm_ref, km_ref, vm_ref,
                      o_ref, lse_ref, ym_ref,
                      sa_ref, sb_ref, ma_ref, mb_ref, pa_ref, pb_ref, sta_ref, stb_ref, *, tq):
    first_tile = pl.program_id(1) == 0
    nsub = tq // BLK

    def window(halo_ref, cur_ref, j, cols):
        if j == 0:
            return jnp.concatenate([halo_ref[:, cols], cur_ref[0:BLK, cols]], axis=0)
        return cur_ref[(j - 1) * BLK:(j + 1) * BLK, cols]

    def scores(j, bufs):
        s_ref, m_ref, _, _ = bufs
        r0 = j * BLK
        sel = jnp.where(first_tile, 0, 1) if j == 0 else 1
        for hp in range(N_PAIRS):
            cols = slice(hp * PAIR, (hp + 1) * PAIR)
            s2 = _pair_scores(q_ref[pl.ds(r0, BLK), cols], window(kh_ref, kc_ref, j, cols))
            s2 = (s2 + bias_ref[sel, hp]).astype(BF16)
            s_ref[hp] = s2
            m_ref[hp] = jnp.broadcast_to(jnp.max(s2, axis=-1, keepdims=True), (2 * BLK, BLK))
        for mp in range(N_MEM_PAIRS):
            cols = slice(mp * PAIR, (mp + 1) * PAIR)
            s2 = _pair_scores(qm_ref[pl.ds(r0, BLK), cols], km_ref[:, cols]).astype(BF16)
            s_ref[N_PAIRS + mp] = s2
            m_ref[N_PAIRS + mp] = jnp.broadcast_to(
                jnp.max(s2, axis=-1, keepdims=True), (2 * BLK, BLK))

    def probs(j, bufs):
        s_ref, m_ref, p_ref, st_ref = bufs
        for u in range(N_PAIRS + N_MEM_PAIRS):
            top = m_ref[u]
            p_ref[u] = jnp.exp2(s_ref[u] - jnp.concatenate([top, top], axis=1))
        for h in range(N_ATTN_HEADS):
            rows = slice((h % 2) * BLK, (h % 2 + 1) * BLK)
            st_ref[0, :, h:h + 1] = m_ref[h // 2, rows, h:h + 1].astype(F32)

    def finish(j, bufs):
        _, _, p_ref, st_ref = bufs
        r0 = j * BLK
        for hp in range(N_PAIRS):
            cols = slice(hp * PAIR, (hp + 1) * PAIR)
            out2 = _pair_pv(p_ref[hp], window(vh_ref, vc_ref, j, cols))
            num, den = _pair_select(out2)
            o_ref[pl.ds(r0, BLK), cols] = (num / den).astype(BF16)
            for hh in range(2):
                rows = slice(hh * BLK, (hh + 1) * BLK)
                h = 2 * hp + hh
                st_ref[1, :, h:h + 1] = out2[rows, PAIR + h:PAIR + h + 1]
        lse_ref[pl.ds(r0, BLK), :] = LN2 * (st_ref[0, :, 0:N_ATTN_HEADS]
                                            + jnp.log2(st_ref[1, :, 0:N_ATTN_HEADS]))
        for mp in range(N_MEM_PAIRS):
            cols = slice(mp * PAIR, (mp + 1) * PAIR)
            num, den = _pair_select(_pair_pv(p_ref[N_PAIRS + mp], vm_ref[:, cols]))
            ym_ref[pl.ds(r0, BLK), cols] = (num / den).astype(BF16)

    assert nsub % 2 == 0 and nsub >= 4
    even = (sa_ref, ma_ref, pa_ref, sta_ref)
    odd = (sb_ref, mb_ref, pb_ref, stb_ref)
    scores(0, even)
    scores(1, odd)
    probs(0, even)

    for n in range(1, nsub - 1, 2):
        scores(n + 1, even)
        probs(n, odd)
        finish(n - 1, even)
        scores(n + 2, odd)
        probs(n + 1, even)
        finish(n, odd)
    probs(nsub - 1, odd)
    finish(nsub - 2, even)
    finish(nsub - 1, odd)


def _band_attn(q, k, v, qm, km, vm, *, tq=2048):
    N, L, _ = q.shape
    M = km.shape[1]
    assert M == 2 * BLK
    sub = tq // BLK
    offs = np.arange(2 * BLK)
    bias = jnp.asarray(_alibi_bias_table(1, BLK + offs[:BLK], offs))
    cur = lambda n, i: (n, i, 0)
    halo = lambda n, i: (n, jnp.maximum(i * sub - 1, 0), 0)
    return pl.pallas_call(
        functools.partial(_band_attn_kernel, tq=tq),
        grid=(N, L // tq),
        in_specs=[
            pl.BlockSpec((None, tq, D_ATTN), cur),
            pl.BlockSpec((None, BLK, D_ATTN), halo),
            pl.BlockSpec((None, tq, D_ATTN), cur),
            pl.BlockSpec((None, BLK, D_ATTN), halo),
            pl.BlockSpec((None, tq, D_ATTN), cur),
            _resident(bias.shape),
            pl.BlockSpec((None, tq, D_XMEM), cur),
            pl.BlockSpec((None, M, D_XMEM), lambda n, i: (n, 0, 0)),
            pl.BlockSpec((None, M, D_XMEM), lambda n, i: (n, 0, 0)),
        ],
        out_specs=[
            pl.BlockSpec((None, tq, D_ATTN), cur),
            pl.BlockSpec((None, tq, N_ATTN_HEADS), cur),
            pl.BlockSpec((None, tq, D_XMEM), cur),
        ],
        out_shape=[
            jax.ShapeDtypeStruct((N, L, D_ATTN), BF16),
            jax.ShapeDtypeStruct((N, L, N_ATTN_HEADS), F32),
            jax.ShapeDtypeStruct((N, L, D_XMEM), BF16),
        ],
        scratch_shapes=[pltpu.VMEM((N_PAIRS + N_MEM_PAIRS, 2 * BLK, 2 * BLK), BF16)] * 2
        + [pltpu.VMEM((N_PAIRS + N_MEM_PAIRS, 2 * BLK, BLK), BF16)] * 2
        + [pltpu.VMEM((N_PAIRS + N_MEM_PAIRS, 2 * BLK, 2 * BLK), BF16)] * 2
        + [pltpu.VMEM((2, BLK, LANES), F32)] * 2,
        compiler_params=pltpu.CompilerParams(
            dimension_semantics=("parallel", "parallel"),
            vmem_limit_bytes=VMEM_LIMIT_BYTES),
    )(q, k, k, v, v, bias, qm, km, vm)


Q4 = BLK // 4


def _dil_attn_kernel(q_ref, kp_ref, kc_ref, vp_ref, vc_ref, b16_ref, b4_ref, unperm_ref,
                     o_ref, lse_ref,
                     sa_ref, sb_ref, ma_ref, mb_ref, pa_ref, pb_ref, ta_ref, tb_ref,
                     n16_ref, d16_ref, m16_ref, op_ref, lp_ref):
    sel_first = jnp.where(pl.program_id(1) == 0, 0, 1)
    is_lo = lax.broadcasted_iota(jnp.int32, (1, PAIR), 1) < HEAD_DIM
    lane = lax.broadcasted_iota(jnp.int32, (1, PAIR), 1)
    cols = lambda hp: slice(hp * PAIR, (hp + 1) * PAIR)

    def scores_into(bufs, q_of, k_of, bias_of):
        s_ref, m_ref, _, _ = bufs
        for hp in range(N_PAIRS):
            s2 = (_pair_scores(q_of(hp), k_of(hp)) + bias_of(hp)).astype(BF16)
            s_ref[hp] = s2
            m_ref[hp] = jnp.broadcast_to(jnp.max(s2, axis=-1, keepdims=True), (2 * BLK, BLK))

    def probs(bufs):
        s_ref, m_ref, p_ref, top_ref = bufs
        for hp in range(N_PAIRS):
            top = m_ref[hp]
            p_ref[hp] = jnp.exp2(s_ref[hp] - jnp.concatenate([top, top], axis=1))
            top_ref[hp] = jnp.where(is_lo, top[:BLK], top[BLK:]).astype(F32)

    def pv_from(bufs, v_of, hp):
        _, _, p_ref, top_ref = bufs
        num, den = _pair_select(_pair_pv(p_ref[hp], v_of(hp)))
        return num, den, top_ref[hp]

    def group_blocks(r4, keep):
        slabs = [r4 + 4 * a for a in range(4)]

        def window16(prev_ref, cur_ref, a):
            return lambda hp: jnp.concatenate(
                [prev_ref[slabs[a], :, cols(hp)], cur_ref[slabs[a], :, cols(hp)]], axis=0)

        def window4(prev_ref, cur_ref, c):
            def rows(hp):
                parts = []
                for s in slabs:
                    if c == 0:
                        parts += [prev_ref[s, BLK - Q4:BLK, cols(hp)], cur_ref[s, 0:Q4, cols(hp)]]
                    else:
                        parts.append(cur_ref[s, (c - 1) * Q4:(c + 1) * Q4, cols(hp)])
                return jnp.concatenate(parts, axis=0)
            return rows

        def quarter_rows(ref, c):
            return lambda hp: jnp.concatenate(
                [ref[s, c * Q4:(c + 1) * Q4, cols(hp)] for s in slabs], axis=0)

        def score16(a, bufs):
            scores_into(bufs, lambda hp: q_ref[slabs[a], :, cols(hp)],
                        window16(kp_ref, kc_ref, a), lambda hp: b16_ref[sel_first, hp])

        def finish16(a, bufs):
            for hp in range(N_PAIRS):
                num, den, top = pv_from(bufs, window16(vp_ref, vc_ref, a), hp)
                n16_ref[keep + a, :, cols(hp)] = num
                d16_ref[keep + a, :, cols(hp)] = den
                m16_ref[keep + a, :, cols(hp)] = top

        def score4(c, bufs):
            sel = sel_first if c == 0 else 1
            scores_into(bufs, quarter_rows(q_ref, c), window4(kp_ref, kc_ref, c),
                        lambda hp: b4_ref[sel, hp])

        def finish4(c, bufs):
            gather = lambda ref, hp: jnp.concatenate(
                [ref[keep + a, c * Q4:(c + 1) * Q4, cols(hp)] for a in range(4)], axis=0)
            lse_tile = jnp.zeros((BLK, PAIR), F32)
            for hp in range(N_PAIRS):
                num4, den4, top4 = pv_from(bufs, window4(vp_ref, vc_ref, c), hp)
                top16 = gather(m16_ref, hp)
                top = jnp.maximum(top16, top4)
                w16 = jnp.exp2(top16 - top)
                w4 = jnp.exp2(top4 - top)
                num = w16 * gather(n16_ref, hp) + w4 * num4
                den = w16 * gather(d16_ref, hp) + w4 * den4
                o = (num / den).astype(BF16)
                lse = LN2 * (top + jnp.log2(den))
                for a, s in enumerate(slabs):
                    op_ref[s, c * Q4:(c + 1) * Q4, cols(hp)] = o[a * Q4:(a + 1) * Q4]
                at = N_ATTN_HEADS * c + 2 * hp
                lse_tile = jnp.where(lane == at, lse, lse_tile)
                lse_tile = jnp.where(lane == at + 1, pltpu.roll(lse, HEAD_DIM, 1), lse_tile)
            for a, s in enumerate(slabs):
                part = lse_tile[a * Q4:(a + 1) * Q4]
                lp_ref[s] = part if c == 0 else lp_ref[s] + part

        return ([(functools.partial(score16, a), functools.partial(finish16, a))
                 for a in range(4)]
                + [(functools.partial(score4, c), functools.partial(finish4, c))
                   for c in range(4)])

    bufs = ((sa_ref, ma_ref, pa_ref, ta_ref), (sb_ref, mb_ref, pb_ref, tb_ref))

    def group(g, carry):
        blocks = group_blocks(2 * g, 0) + group_blocks(2 * g + 1, 4)
        blocks[0][0](bufs[0])
        for n in range(len(blocks) + 1):
            if n + 1 < len(blocks):
                blocks[n + 1][0](bufs[(n + 1) % 2])
            if n < len(blocks):
                probs(bufs[n % 2])
            if n >= 1:
                blocks[n - 1][1](bufs[(n - 1) % 2])
        return carry

    lax.fori_loop(0, 2, group, 0)

    unperm = unperm_ref[...]
    for n in range(BLK // PERM_PER):
        rows = slice(n * PERM_PER, (n + 1) * PERM_PER)
        o_res = jnp.concatenate([op_ref[r, rows, :] for r in range(N_RES)], axis=0)
        o_ref[n * PERM_ROWS:(n + 1) * PERM_ROWS, :] = jnp.dot(
            unperm, o_res, preferred_element_type=F32).astype(BF16)
    for part in range(Q4 // PERM_PER):
        rows = slice(part * PERM_PER, (part + 1) * PERM_PER)
        l_res = jnp.concatenate([lp_ref[r, rows, :] for r in range(N_RES)], axis=0)
        hi = l_res.astype(BF16)
        rest = l_res - hi.astype(F32)
        mid = rest.astype(BF16)
        low = (rest - mid.astype(F32)).astype(BF16)
        l_nat = (jnp.dot(unperm, hi, preferred_element_type=F32)
                 + jnp.dot(unperm, mid, preferred_element_type=F32)
                 + jnp.dot(unperm, low, preferred_element_type=F32))
        for c in range(BLK // Q4):
            first = (c * Q4 + part * PERM_PER) * N_RES
            lse_ref[first:first + PERM_ROWS, :] = l_nat[:, N_ATTN_HEADS * c:N_ATTN_HEADS * (c + 1)]


def _dil_attn(qr, kr, vr):
    B, T = qr.shape[:2]
    S = T * TILE
    offs = np.arange(2 * BLK)
    b16 = jnp.asarray(_alibi_bias_table(16, BLK + offs[:BLK], offs))
    qo = np.arange(BLK)
    b4 = jnp.asarray(_alibi_bias_table(
        4, BLK + 4 * (qo % Q4) + qo // Q4, 4 * (offs % (2 * Q4)) + offs // (2 * Q4)))
    unperm = jnp.asarray(_residue_permutation(PERM_ROWS).T, dtype=BF16)
    slab_block = (None, None, N_RES, BLK, D_ATTN)
    cur = lambda b, t: (b, t, 0, 0, 0)
    prev = lambda b, t: (b, jnp.maximum(t - 1, 0), 0, 0, 0)
    nat = lambda b, t: (b, t, 0)
    return pl.pallas_call(
        _dil_attn_kernel,
        grid=(B, T),
        in_specs=[
            pl.BlockSpec(slab_block, cur),
            pl.BlockSpec(slab_block, prev),
            pl.BlockSpec(slab_block, cur),
            pl.BlockSpec(slab_block, prev),
            pl.BlockSpec(slab_block, cur),
            _resident(b16.shape),
            _resident(b4.shape),
            _resident(unperm.shape),
        ],
        out_specs=[
            pl.BlockSpec((None, TILE, D_ATTN), nat),
            pl.BlockSpec((None, TILE, N_ATTN_HEADS), nat),
        ],
        out_shape=[
            jax.ShapeDtypeStruct((B, S, D_ATTN), BF16),
            jax.ShapeDtypeStruct((B, S, N_ATTN_HEADS), F32),
        ],
        scratch_shapes=[pltpu.VMEM((N_PAIRS, 2 * BLK, 2 * BLK), BF16)] * 2
        + [pltpu.VMEM((N_PAIRS, 2 * BLK, BLK), BF16)] * 2
        + [pltpu.VMEM((N_PAIRS, 2 * BLK, 2 * BLK), BF16)] * 2
        + [pltpu.VMEM((N_PAIRS, BLK, PAIR), F32)] * 2
        + [pltpu.VMEM((2 * 4, BLK, D_ATTN), F32)] * 3
        + [pltpu.VMEM((N_RES, BLK, D_ATTN), BF16), pltpu.VMEM((N_RES, Q4, PAIR), F32)],
        compiler_params=pltpu.CompilerParams(
            dimension_semantics=("parallel", "parallel"),
            vmem_limit_bytes=VMEM_LIMIT_BYTES),
    )(qr, kr, kr, vr, vr, b16, b4, unperm)


def _out_ffn_kernel(x_ref, o1_ref, oa_ref, l1_ref, la_ref,
                    yp_ref, ym_ref, wout_ref, gffn_ref, w1_ref, w2_ref,
                    gfin_ref, out_ref, *, sub, ff_chunk):
    def rows_of(rows):
        l1, la = l1_ref[rows, :], la_ref[rows, :]
        top = jnp.maximum(l1, la)
        e1, ea = jnp.exp(l1 - top), jnp.exp(la - top)
        w1 = e1 / (e1 + ea)
        is_lo = lax.broadcasted_iota(jnp.int32, (1, PAIR), 1) < HEAD_DIM
        w1 = jnp.concatenate(
            [jnp.where(is_lo, w1[:, 2 * hp:2 * hp + 1], w1[:, 2 * hp + 1:2 * hp + 2])
             for hp in range(N_PAIRS)], axis=1)
        oa = oa_ref[rows, :].astype(F32)
        y_attn = oa + w1 * (o1_ref[rows, :].astype(F32) - oa)

        mix = jnp.dot(y_attn.astype(BF16), wout_ref[0:D_ATTN, :], preferred_element_type=F32)
        mix = mix + jnp.dot(yp_ref[rows, :], wout_ref[D_ATTN:D_ATTN + D_POOL, :],
                            preferred_element_type=F32)
        mix = mix + jnp.dot(ym_ref[rows, :], wout_ref[D_ATTN + D_POOL:, :],
                            preferred_element_type=F32)
        x1 = x_ref[rows, :] + mix

        h2 = _rmsnorm(x1, gffn_ref[...]).astype(BF16)
        ff = jnp.zeros(x1.shape, F32)
        for c in range(D_FF // ff_chunk):
            a = jnp.dot(h2, w1_ref[:, c * ff_chunk:(c + 1) * ff_chunk],
                        preferred_element_type=F32)
            a = jnp.maximum(a, 0.0)
            ff = ff + jnp.dot((a * a).astype(BF16), w2_ref[c * ff_chunk:(c + 1) * ff_chunk, :],
                              preferred_element_type=F32)
        out_ref[rows, :] = _rmsnorm(x1 + ff, gfin_ref[...])

    for i in range(x_ref.shape[0] // sub):
        rows_of(slice(i * sub, (i + 1) * sub))


def _out_ffn(x, o_list, lse_list, y_pool, y_mem, w_out, g_ffn, w_ff1, w_ff2, g_final,
             *, tm=1024, sub=512, ff_chunk=1024):
    B, S, _ = x.shape
    row = lambda b, t: (b, t, 0)
    tile = lambda width: pl.BlockSpec((None, tm, width), row)
    return pl.pallas_call(
        functools.partial(_out_ffn_kernel, sub=sub, ff_chunk=ff_chunk),
        grid=(B, S // tm),
        in_specs=[
            tile(D_MODEL),
            tile(D_ATTN), tile(D_ATTN),
            tile(N_ATTN_HEADS), tile(N_ATTN_HEADS),
            tile(D_POOL), tile(D_XMEM),
            _resident(w_out.shape),
            _resident((1, D_MODEL)),
            _resident(w_ff1.shape),
            _resident(w_ff2.shape),
            _resident((1, D_MODEL)),
        ],
        out_specs=tile(D_MODEL),
        out_shape=jax.ShapeDtypeStruct((B, S, D_MODEL), F32),
        compiler_params=pltpu.CompilerParams(
            dimension_semantics=("parallel", "parallel"),
            vmem_limit_bytes=VMEM_LIMIT_BYTES),
    )(x, *o_list, *lse_list, y_pool, y_mem, w_out, g_ffn, w_ff1, w_ff2, g_final)


def _block_diag(w_pool):
    G, C, E = w_pool.shape
    out = jnp.zeros((G * C, G * E), w_pool.dtype)
    for g in range(G):
        out = out.at[g * C:(g + 1) * C, g * E:(g + 1) * E].set(w_pool[g])
    return out


def kernel(x, mem, g_mix, w_in, g_mem, w_mem_kv, w_pool, pool_scale, w_out,
           g_ffn, w_ff1, w_ff2, g_final):
    depth = w_in.shape[0]
    for i in range(depth):
        km, vm = _memkv(mem, g_mem[i][None], w_mem_kv[i].astype(BF16))
        q, k, v, qr, kr, vr, qm, y_pool, w_out16, w_ff1_16, w_ff2_16 = _inproj(
            x, g_mix[i][None], w_in[i].astype(BF16),
            _block_diag(w_pool[i]).astype(BF16), pool_scale[i][None],
            (w_out[i], w_ff1[i], w_ff2[i]))

        o1, lse1, y_mem = _band_attn(q, k, v, qm, km, vm)
        oa, lsea = _dil_attn(qr, kr, vr)

        assert depth == 1
        x = _out_ffn(x, [o1, oa], [lse1, lsea], y_pool, y_mem, w_out16,
                     g_ffn[i][None], w_ff1_16, w_ff2_16, g_final[None])
    return x
```

```python
import functools

import numpy as np
import jax
import jax.numpy as jnp
from jax import lax
from jax.experimental import pallas as pl
from jax.experimental.pallas import tpu as pltpu

D_MODEL = 1024
HEAD_DIM = 64
N_ATTN_HEADS = 8
N_MEM_HEADS = 4
POOL_WINDOWS = (2, 4, 8, 16)
POOL_GROUP_DIM = 64
D_ATTN = N_ATTN_HEADS * HEAD_DIM
D_POOL = len(POOL_WINDOWS) * POOL_GROUP_DIM
D_XMEM = N_MEM_HEADS * HEAD_DIM
DILATED = ((128, 1), (512, 4), (2048, 16))
BLK = 128
SPAN = 128
D_FF = 4 * D_MODEL
EPS = 1e-6
POOL_HALO = 16

N_RES = 16
TILE = N_RES * BLK
PAIR = 2 * HEAD_DIM
LANES = 128
PERM_ROWS = 256
PERM_PER = PERM_ROWS // N_RES
N_PAIRS = N_ATTN_HEADS // 2
N_MEM_PAIRS = N_MEM_HEADS // 2

BF16 = jnp.bfloat16
F32 = jnp.float32
LOG2E = 1.4426950408889634
LN2 = 0.6931471805599453

VMEM_LIMIT_BYTES = 56 * 1024 * 1024

assert all(w // d == SPAN for w, d in DILATED)
assert [d for _, d in DILATED] == [1, 4, N_RES]
assert max(POOL_WINDOWS) <= POOL_HALO
assert all(w & (w - 1) == 0 for w in POOL_WINDOWS) and list(POOL_WINDOWS) == sorted(POOL_WINDOWS)


def _rmsnorm(x, g):
    ms = jnp.mean(x * x, axis=-1, keepdims=True)
    return (x * lax.rsqrt(ms + EPS)) * g


def _resident(shape):
    return pl.BlockSpec(shape, lambda *_: (0,) * len(shape), pipeline_mode=pl.Buffered(1))


def _residue_permutation(rows):
    per = rows // N_RES
    p = np.zeros((rows, rows), np.float32)
    for r in range(N_RES):
        for i in range(per):
            p[r * per + i, N_RES * i + r] = 1.0
    return p


def _inproj_kernel(x_ref, g_ref, w_ref, wpool_ref, pscale_ref, perm_ref, la_ref, lb_ref, lc_ref,
                   q_ref, k_ref, v_ref, qr_ref, kr_ref, vr_ref, qm_ref, yp_ref,
                   la16_ref, lb16_ref, lc16_ref, carry_ref, h_ref, *, tm, sub):
    t = pl.program_id(1)
    per = sub // N_RES

    for src_ref, dst_ref in ((la_ref, la16_ref), (lb_ref, lb16_ref), (lc_ref, lc16_ref)):
        dst_ref[...] = src_ref[...].astype(BF16)

    @pl.when(t == 0)
    def _():
        carry_ref[...] = jnp.zeros((POOL_HALO, D_POOL), F32)

    def rows_of(i):
        rows = slice(i * sub, (i + 1) * sub)
        h_ref[rows, :] = _rmsnorm(x_ref[rows, :], g_ref[...]).astype(BF16)

        def proj(lo, hi):
            return jnp.dot(h_ref[rows, :], w_ref[:, lo:hi], preferred_element_type=F32)

        u = proj(3 * D_ATTN, 3 * D_ATTN + D_POOL)
        run = jnp.concatenate([carry_ref[...], u], axis=0)
        carry_ref[...] = u[sub - POOL_HALO:, :]
        group = lax.broadcasted_iota(jnp.int32, (1, D_POOL), 1) // POOL_GROUP_DIM
        total = None
        width = 1
        for g, w in enumerate(POOL_WINDOWS):
            while width < w:
                run = run + pltpu.roll(run, width, 0)
                width *= 2
            total = run if total is None else jnp.where(group >= g, run, total)
        total = total[POOL_HALO:, :]
        win = jnp.zeros((1, D_POOL), jnp.int32)
        for g, w in enumerate(POOL_WINDOWS):
            win = jnp.where(group == g, w, win)
        pos = t * tm + i * sub + lax.broadcasted_iota(jnp.int32, (sub, 1), 0)
        cnt = jnp.minimum(win, pos + 1).astype(F32)
        d = (total / cnt - u).astype(BF16)

        scale = LOG2E * HEAD_DIM ** -0.5
        q_ref[rows, :] = (proj(0, D_ATTN) * scale).astype(BF16)
        k_ref[rows, :] = proj(D_ATTN, 2 * D_ATTN).astype(BF16)
        v_ref[rows, :] = proj(2 * D_ATTN, 3 * D_ATTN).astype(BF16)
        qm_ref[rows, :] = (proj(3 * D_ATTN + D_POOL, 3 * D_ATTN + D_POOL + D_XMEM)
                           * scale).astype(BF16)

        yp = jnp.dot(d, wpool_ref[...], preferred_element_type=F32) * pscale_ref[...]
        yp_ref[rows, :] = yp.astype(BF16)

        perm = perm_ref[...]
        for nat_ref, res_ref in ((q_ref, qr_ref), (k_ref, kr_ref), (v_ref, vr_ref)):
            for piece in range(sub // PERM_ROWS):
                src = slice(i * sub + piece * PERM_ROWS, i * sub + (piece + 1) * PERM_ROWS)
                dst = slice(i * per + piece * PERM_PER, i * per + (piece + 1) * PERM_PER)
                moved = jnp.dot(perm, nat_ref[src, :], preferred_element_type=F32).astype(BF16)
                res_ref[:, dst, :] = moved.reshape(N_RES, PERM_PER, D_ATTN)

    for i in range(tm // sub):
        rows_of(i)


def _inproj(x, g_mix, w_in, wpool_bd, pool_scale, later_weights, *, tm=1024, sub=512):
    B, S, _ = x.shape
    d_in = w_in.shape[1]
    per_tile = TILE // tm
    steps = B * (S // tm)
    chunk = lambda b, t: (b * (S // tm) + t, 0)
    later_specs = [pl.BlockSpec((w.shape[0] // steps, w.shape[1]), chunk) for w in later_weights]
    assert all(w.shape[0] % (steps * 16) == 0 for w in later_weights)
    perm = jnp.asarray(_residue_permutation(PERM_ROWS), dtype=BF16)
    row = lambda b, t: (b, t, 0)
    res = lambda b, t: (b, t // per_tile, 0, t % per_tile, 0)
    nat_spec = pl.BlockSpec((None, tm, D_ATTN), row)
    res_spec = pl.BlockSpec((None, None, N_RES, tm // N_RES, D_ATTN), res)
    nat_shape = jax.ShapeDtypeStruct((B, S, D_ATTN), BF16)
    res_shape = jax.ShapeDtypeStruct((B, S // TILE, N_RES, BLK, D_ATTN), BF16)
    return pl.pallas_call(
        functools.partial(_inproj_kernel, tm=tm, sub=sub),
        grid=(B, S // tm),
        in_specs=[
            pl.BlockSpec((None, tm, D_MODEL), row),
            _resident((1, D_MODEL)),
            _resident((D_MODEL, d_in)),
            _resident((D_POOL, D_POOL)),
            _resident((1, D_POOL)),
            _resident((PERM_ROWS, PERM_ROWS)),
            *later_specs,
        ],
        out_specs=[
            nat_spec, nat_spec, nat_spec, res_spec, res_spec, res_spec,
            pl.BlockSpec((None, tm, D_XMEM), row),
            pl.BlockSpec((None, tm, D_POOL), row),
            *later_specs,
        ],
        out_shape=[
            nat_shape, nat_shape, nat_shape, res_shape, res_shape, res_shape,
            jax.ShapeDtypeStruct((B, S, D_XMEM), BF16),
            jax.ShapeDtypeStruct((B, S, D_POOL), BF16),
            *[jax.ShapeDtypeStruct(w.shape, BF16) for w in later_weights],
        ],
        scratch_shapes=[pltpu.VMEM((POOL_HALO, D_POOL), F32),
                        pltpu.VMEM((tm, D_MODEL), BF16)],
        compiler_params=pltpu.CompilerParams(
            dimension_semantics=("arbitrary", "arbitrary"),
            vmem_limit_bytes=VMEM_LIMIT_BYTES),
    )(x, g_mix, w_in, wpool_bd, pool_scale, perm, *later_weights)


def _alibi_bias_table(dilation, q_off, k_off):
    slopes = (2.0 ** (-8.0 * np.arange(1, N_ATTN_HEADS + 1, dtype=np.float32)
                      / N_ATTN_HEADS)).astype(np.float32)
    rel = q_off[:, None] - k_off[None, :]
    valid = (rel >= 0) & (rel <= SPAN)
    bias = -slopes[:, None, None] * (dilation * rel).astype(np.float32)[None] * np.float32(LOG2E)
    full = np.where(valid[None], bias, -np.inf)
    first = np.where((valid & (k_off[None, :] >= BLK))[None], bias, -np.inf)
    table = np.stack([first, full]).astype(np.float32)
    return table.reshape(2, N_PAIRS, 2 * BLK, 2 * BLK)


def _pair_scores(qp, kp):
    is_lo = lax.broadcasted_iota(jnp.int32, (1, PAIR), 1) < HEAD_DIM
    zero = jnp.zeros_like(qp)
    q2 = jnp.concatenate([jnp.where(is_lo, qp, zero), jnp.where(is_lo, zero, qp)], axis=0)
    return lax.dot_general(q2, kp, (((1,), (1,)), ((), ())), preferred_element_type=F32)


def _pair_pv(p2, vp):
    vaug = jnp.concatenate([vp, jnp.ones_like(vp)], axis=1)
    return jnp.dot(p2.astype(BF16), vaug, preferred_element_type=F32)


def _pair_select(out2):
    R = out2.shape[0] // 2
    is_lo = lax.broadcasted_iota(jnp.int32, (1, PAIR), 1) < HEAD_DIM
    num = jnp.where(is_lo, out2[:R, :PAIR], out2[R:, :PAIR])
    den = jnp.where(is_lo, out2[:R, PAIR:], out2[R:, PAIR:])
    return num, den


def _band_attn_kernel(q_ref, kh_ref, kc_ref, vh_ref, vc_ref, bias_ref, qm_ref,
                      mem_ref, gmem_ref, wmem_ref,
                      o_ref, lse_ref, ym_ref,
                      sa_ref, sb_ref, ma_ref, mb_ref, pa_ref, pb_ref, sta_ref, stb_ref,
                      km_ref, vm_ref, *, tq):
    first_tile = pl.program_id(1) == 0
    nsub = tq // BLK

    @pl.when(first_tile)
    def _():
        mn = _rmsnorm(mem_ref[...], gmem_ref[...]).astype(BF16)
        kv = jnp.dot(mn, wmem_ref[...].astype(BF16), preferred_element_type=F32)
        km_ref[...] = kv[:, :D_XMEM].astype(BF16)
        vm_ref[...] = kv[:, D_XMEM:].astype(BF16)

    def window(halo_ref, cur_ref, j, cols):
        if j == 0:
            return jnp.concatenate([halo_ref[:, cols], cur_ref[0:BLK, cols]], axis=0)
        return cur_ref[(j - 1) * BLK:(j + 1) * BLK, cols]

    def scores(j, bufs):
        s_ref, m_ref, _, _ = bufs
        r0 = j * BLK
        sel = jnp.where(first_tile, 0, 1) if j == 0 else 1
        for hp in range(N_PAIRS):
            cols = slice(hp * PAIR, (hp + 1) * PAIR)
            s2 = _pair_scores(q_ref[pl.ds(r0, BLK), cols], window(kh_ref, kc_ref, j, cols))
            s2 = s2 + bias_ref[sel, hp]
            s_ref[hp] = s2
            m_ref[hp] = jnp.broadcast_to(jnp.max(s2, axis=-1, keepdims=True), (2 * BLK, BLK))
        for mp in range(N_MEM_PAIRS):
            cols = slice(mp * PAIR, (mp + 1) * PAIR)
            s2 = _pair_scores(qm_ref[pl.ds(r0, BLK), cols], km_ref[:, cols])
            s_ref[N_PAIRS + mp] = s2
            m_ref[N_PAIRS + mp] = jnp.broadcast_to(
                jnp.max(s2, axis=-1, keepdims=True), (2 * BLK, BLK))

    def probs(j, bufs):
        s_ref, m_ref, p_ref, st_ref = bufs
        for u in range(N_PAIRS + N_MEM_PAIRS):
            top = m_ref[u]
            p_ref[u] = jnp.exp2(s_ref[u] - jnp.concatenate([top, top], axis=1)).astype(BF16)
        for h in range(N_ATTN_HEADS):
            rows = slice((h % 2) * BLK, (h % 2 + 1) * BLK)
            st_ref[0, :, h:h + 1] = m_ref[h // 2, rows, h:h + 1]

    def finish(j, bufs):
        _, _, p_ref, st_ref = bufs
        r0 = j * BLK
        for hp in range(N_PAIRS):
            cols = slice(hp * PAIR, (hp + 1) * PAIR)
            out2 = _pair_pv(p_ref[hp], window(vh_ref, vc_ref, j, cols))
            num, den = _pair_select(out2)
            o_ref[pl.ds(r0, BLK), cols] = (num / den).astype(BF16)
            for hh in range(2):
                rows = slice(hh * BLK, (hh + 1) * BLK)
                h = 2 * hp + hh
                st_ref[1, :, h:h + 1] = out2[rows, PAIR + h:PAIR + h + 1]
        lse_ref[pl.ds(r0, BLK), :] = LN2 * (st_ref[0, :, 0:N_ATTN_HEADS]
                                            + jnp.log2(st_ref[1, :, 0:N_ATTN_HEADS]))
        for mp in range(N_MEM_PAIRS):
            cols = slice(mp * PAIR, (mp + 1) * PAIR)
            num, den = _pair_select(_pair_pv(p_ref[N_PAIRS + mp], vm_ref[:, cols]))
            ym_ref[pl.ds(r0, BLK), cols] = (num / den).astype(BF16)

    assert nsub % 2 == 0 and nsub >= 4
    even = (sa_ref, ma_ref, pa_ref, sta_ref)
    odd = (sb_ref, mb_ref, pb_ref, stb_ref)
    scores(0, even)
    scores(1, odd)
    probs(0, even)

    for n in range(1, nsub - 1, 2):
        scores(n + 1, even)
        probs(n, odd)
        finish(n - 1, even)
        scores(n + 2, odd)
        probs(n + 1, even)
        finish(n, odd)
    probs(nsub - 1, odd)
    finish(nsub - 2, even)
    finish(nsub - 1, odd)


def _band_attn(q, k, v, qm, mem, g_mem, w_mem_kv, *, tq=2048):
    N, L, _ = q.shape
    M = mem.shape[1]
    assert M == 2 * BLK
    sub = tq // BLK
    offs = np.arange(2 * BLK)
    bias = jnp.asarray(_alibi_bias_table(1, BLK + offs[:BLK], offs))
    cur = lambda n, i: (n, i, 0)
    halo = lambda n, i: (n, jnp.maximum(i * sub - 1, 0), 0)
    return pl.pallas_call(
        functools.partial(_band_attn_kernel, tq=tq),
        grid=(N, L // tq),
        in_specs=[
            pl.BlockSpec((None, tq, D_ATTN), cur),
            pl.BlockSpec((None, BLK, D_ATTN), halo),
            pl.BlockSpec((None, tq, D_ATTN), cur),
            pl.BlockSpec((None, BLK, D_ATTN), halo),
            pl.BlockSpec((None, tq, D_ATTN), cur),
            _resident(bias.shape),
            pl.BlockSpec((None, tq, D_XMEM), cur),
            pl.BlockSpec((None, M, D_MODEL), lambda n, i: (n, 0, 0)),
            _resident((1, D_MODEL)),
            _resident((D_MODEL, 2 * D_XMEM)),
        ],
        out_specs=[
            pl.BlockSpec((None, tq, D_ATTN), cur),
            pl.BlockSpec((None, tq, N_ATTN_HEADS), cur),
            pl.BlockSpec((None, tq, D_XMEM), cur),
        ],
        out_shape=[
            jax.ShapeDtypeStruct((N, L, D_ATTN), BF16),
            jax.ShapeDtypeStruct((N, L, N_ATTN_HEADS), F32),
            jax.ShapeDtypeStruct((N, L, D_XMEM), BF16),
        ],
        scratch_shapes=[pltpu.VMEM((N_PAIRS + N_MEM_PAIRS, 2 * BLK, 2 * BLK), F32)] * 2
        + [pltpu.VMEM((N_PAIRS + N_MEM_PAIRS, 2 * BLK, BLK), F32)] * 2
        + [pltpu.VMEM((N_PAIRS + N_MEM_PAIRS, 2 * BLK, 2 * BLK), BF16)] * 2
        + [pltpu.VMEM((2, BLK, LANES), F32)] * 2
        + [pltpu.VMEM((M, D_XMEM), BF16)] * 2,
        compiler_params=pltpu.CompilerParams(
            dimension_semantics=("arbitrary", "arbitrary"),
            vmem_limit_bytes=VMEM_LIMIT_BYTES),
    )(q, k, k, v, v, bias, qm, mem, g_mem, w_mem_kv)


Q4 = BLK // 4


def _dil_attn_kernel(q_ref, kp_ref, kc_ref, vp_ref, vc_ref, b16_ref, b4_ref, unperm_ref,
                     o_ref, lse_ref,
                     sa_ref, sb_ref, ma_ref, mb_ref, pa_ref, pb_ref, ta_ref, tb_ref,
                     n16_ref, d16_ref, m16_ref, op_ref, lp_ref):
    sel_first = jnp.where(pl.program_id(1) == 0, 0, 1)
    is_lo = lax.broadcasted_iota(jnp.int32, (1, PAIR), 1) < HEAD_DIM
    lane = lax.broadcasted_iota(jnp.int32, (1, PAIR), 1)
    cols = lambda hp: slice(hp * PAIR, (hp + 1) * PAIR)

    def scores_into(bufs, q_of, k_of, bias_of):
        s_ref, m_ref, _, _ = bufs
        for hp in range(N_PAIRS):
            s2 = _pair_scores(q_of(hp), k_of(hp)) + bias_of(hp)
            s_ref[hp] = s2
            m_ref[hp] = jnp.broadcast_to(jnp.max(s2, axis=-1, keepdims=True), (2 * BLK, BLK))

    def probs(bufs):
        s_ref, m_ref, p_ref, top_ref = bufs
        for hp in range(N_PAIRS):
            top = m_ref[hp]
            p_ref[hp] = jnp.exp2(s_ref[hp] - jnp.concatenate([top, top], axis=1)).astype(BF16)
            top_ref[hp] = jnp.where(is_lo, top[:BLK], top[BLK:])

    def pv_from(bufs, v_of, hp):
        _, _, p_ref, top_ref = bufs
        num, den = _pair_select(_pair_pv(p_ref[hp], v_of(hp)))
        return num, den, top_ref[hp]

    def group_blocks(r4, keep):
        slabs = [r4 + 4 * a for a in range(4)]

        def window16(prev_ref, cur_ref, a):
            return lambda hp: jnp.concatenate(
                [prev_ref[slabs[a], :, cols(hp)], cur_ref[slabs[a], :, cols(hp)]], axis=0)

        def window4(prev_ref, cur_ref, c):
            def rows(hp):
                parts = []
                for s in slabs:
                    if c == 0:
                        parts += [prev_ref[s, BLK - Q4:BLK, cols(hp)], cur_ref[s, 0:Q4, cols(hp)]]
                    else:
                        parts.append(cur_ref[s, (c - 1) * Q4:(c + 1) * Q4, cols(hp)])
                return jnp.concatenate(parts, axis=0)
            return rows

        def quarter_rows(ref, c):
            return lambda hp: jnp.concatenate(
                [ref[s, c * Q4:(c + 1) * Q4, cols(hp)] for s in slabs], axis=0)

        def score16(a, bufs):
            scores_into(bufs, lambda hp: q_ref[slabs[a], :, cols(hp)],
                        window16(kp_ref, kc_ref, a), lambda hp: b16_ref[sel_first, hp])

        def finish16(a, bufs):
            for hp in range(N_PAIRS):
                num, den, top = pv_from(bufs, window16(vp_ref, vc_ref, a), hp)
                n16_ref[keep + a, :, cols(hp)] = num
                d16_ref[keep + a, :, cols(hp)] = den
                m16_ref[keep + a, :, cols(hp)] = top

        def score4(c, bufs):
            sel = sel_first if c == 0 else 1
            scores_into(bufs, quarter_rows(q_ref, c), window4(kp_ref, kc_ref, c),
                        lambda hp: b4_ref[sel, hp])

        def finish4(c, bufs):
            gather = lambda ref, hp: jnp.concatenate(
                [ref[keep + a, c * Q4:(c + 1) * Q4, cols(hp)] for a in range(4)], axis=0)
            lse_tile = jnp.zeros((BLK, PAIR), F32)
            for hp in range(N_PAIRS):
                num4, den4, top4 = pv_from(bufs, window4(vp_ref, vc_ref, c), hp)
                top16 = gather(m16_ref, hp)
                top = jnp.maximum(top16, top4)
                w16 = jnp.exp2(top16 - top)
                w4 = jnp.exp2(top4 - top)
                num = w16 * gather(n16_ref, hp) + w4 * num4
                den = w16 * gather(d16_ref, hp) + w4 * den4
                o = (num / den).astype(BF16)
                lse = LN2 * (top + jnp.log2(den))
                for a, s in enumerate(slabs):
                    op_ref[s, c * Q4:(c + 1) * Q4, cols(hp)] = o[a * Q4:(a + 1) * Q4]
                at = N_ATTN_HEADS * c + 2 * hp
                lse_tile = jnp.where(lane == at, lse, lse_tile)
                lse_tile = jnp.where(lane == at + 1, pltpu.roll(lse, HEAD_DIM, 1), lse_tile)
            for a, s in enumerate(slabs):
                part = lse_tile[a * Q4:(a + 1) * Q4]
                lp_ref[s] = part if c == 0 else lp_ref[s] + part

        return ([(functools.partial(score16, a), functools.partial(finish16, a))
                 for a in range(4)]
                + [(functools.partial(score4, c), functools.partial(finish4, c))
                   for c in range(4)])

    bufs = ((sa_ref, ma_ref, pa_ref, ta_ref), (sb_ref, mb_ref, pb_ref, tb_ref))

    def group(g, carry):
        blocks = group_blocks(2 * g, 0) + group_blocks(2 * g + 1, 4)
        blocks[0][0](bufs[0])
        for n in range(len(blocks) + 1):
            if n + 1 < len(blocks):
                blocks[n + 1][0](bufs[(n + 1) % 2])
            if n < len(blocks):
                probs(bufs[n % 2])
            if n >= 1:
                blocks[n - 1][1](bufs[(n - 1) % 2])
        return carry

    lax.fori_loop(0, 2, group, 0)

    unperm = unperm_ref[...]
    for n in range(BLK // PERM_PER):
        rows = slice(n * PERM_PER, (n + 1) * PERM_PER)
        o_res = jnp.concatenate([op_ref[r, rows, :] for r in range(N_RES)], axis=0)
        o_ref[n * PERM_ROWS:(n + 1) * PERM_ROWS, :] = jnp.dot(
            unperm, o_res, preferred_element_type=F32).astype(BF16)
    for part in range(Q4 // PERM_PER):
        rows = slice(part * PERM_PER, (part + 1) * PERM_PER)
        l_res = jnp.concatenate([lp_ref[r, rows, :] for r in range(N_RES)], axis=0)
        hi = l_res.astype(BF16)
        rest = l_res - hi.astype(F32)
        mid = rest.astype(BF16)
        low = (rest - mid.astype(F32)).astype(BF16)
        l_nat = (jnp.dot(unperm, hi, preferred_element_type=F32)
                 + jnp.dot(unperm, mid, preferred_element_type=F32)
                 + jnp.dot(unperm, low, preferred_element_type=F32))
        for c in range(BLK // Q4):
            first = (c * Q4 + part * PERM_PER) * N_RES
            lse_ref[first:first + PERM_ROWS, :] = l_nat[:, N_ATTN_HEADS * c:N_ATTN_HEADS * (c + 1)]


def _dil_attn(qr, kr, vr):
    B, T = qr.shape[:2]
    S = T * TILE
    offs = np.arange(2 * BLK)
    b16 = jnp.asarray(_alibi_bias_table(16, BLK + offs[:BLK], offs))
    qo = np.arange(BLK)
    b4 = jnp.asarray(_alibi_bias_table(
        4, BLK + 4 * (qo % Q4) + qo // Q4, 4 * (offs % (2 * Q4)) + offs // (2 * Q4)))
    unperm = jnp.asarray(_residue_permutation(PERM_ROWS).T, dtype=BF16)
    slab_block = (None, None, N_RES, BLK, D_ATTN)
    cur = lambda b, t: (b, t, 0, 0, 0)
    prev = lambda b, t: (b, jnp.maximum(t - 1, 0), 0, 0, 0)
    nat = lambda b, t: (b, t, 0)
    return pl.pallas_call(
        _dil_attn_kernel,
        grid=(B, T),
        in_specs=[
            pl.BlockSpec(slab_block, cur),
            pl.BlockSpec(slab_block, prev),
            pl.BlockSpec(slab_block, cur),
            pl.BlockSpec(slab_block, prev),
            pl.BlockSpec(slab_block, cur),
            _resident(b16.shape),
            _resident(b4.shape),
            _resident(unperm.shape),
        ],
        out_specs=[
            pl.BlockSpec((None, TILE, D_ATTN), nat),
            pl.BlockSpec((None, TILE, N_ATTN_HEADS), nat),
        ],
        out_shape=[
            jax.ShapeDtypeStruct((B, S, D_ATTN), BF16),
            jax.ShapeDtypeStruct((B, S, N_ATTN_HEADS), F32),
        ],
        scratch_shapes=[pltpu.VMEM((N_PAIRS, 2 * BLK, 2 * BLK), F32)] * 2
        + [pltpu.VMEM((N_PAIRS, 2 * BLK, BLK), F32)] * 2
        + [pltpu.VMEM((N_PAIRS, 2 * BLK, 2 * BLK), BF16)] * 2
        + [pltpu.VMEM((N_PAIRS, BLK, PAIR), F32)] * 2
        + [pltpu.VMEM((2 * 4, BLK, D_ATTN), F32)] * 3
        + [pltpu.VMEM((N_RES, BLK, D_ATTN), BF16), pltpu.VMEM((N_RES, Q4, PAIR), F32)],
        compiler_params=pltpu.CompilerParams(
            dimension_semantics=("parallel", "parallel"),
            vmem_limit_bytes=VMEM_LIMIT_BYTES),
    )(qr, kr, kr, vr, vr, b16, b4, unperm)


def _out_ffn_kernel(x_ref, o1_ref, oa_ref, l1_ref, la_ref,
                    yp_ref, ym_ref, wout_ref, gffn_ref, w1_ref, w2_ref,
                    gfin_ref, out_ref, *, sub, ff_chunk):
    def rows_of(rows):
        l1, la = l1_ref[rows, :], la_ref[rows, :]
        top = jnp.maximum(l1, la)
        e1, ea = jnp.exp(l1 - top), jnp.exp(la - top)
        w1 = e1 / (e1 + ea)
        is_lo = lax.broadcasted_iota(jnp.int32, (1, PAIR), 1) < HEAD_DIM
        w1 = jnp.concatenate(
            [jnp.where(is_lo, w1[:, 2 * hp:2 * hp + 1], w1[:, 2 * hp + 1:2 * hp + 2])
             for hp in range(N_PAIRS)], axis=1)
        oa = oa_ref[rows, :].astype(F32)
        y_attn = oa + w1 * (o1_ref[rows, :].astype(F32) - oa)

        mix = jnp.dot(y_attn.astype(BF16), wout_ref[0:D_ATTN, :], preferred_element_type=F32)
        mix = mix + jnp.dot(yp_ref[rows, :], wout_ref[D_ATTN:D_ATTN + D_POOL, :],
                            preferred_element_type=F32)
        mix = mix + jnp.dot(ym_ref[rows, :], wout_ref[D_ATTN + D_POOL:, :],
                            preferred_element_type=F32)
        x1 = x_ref[rows, :] + mix

        h2 = _rmsnorm(x1, gffn_ref[...]).astype(BF16)
        ff = jnp.zeros(x1.shape, F32)
        for c in range(D_FF // ff_chunk):
            a = jnp.dot(h2, w1_ref[:, c * ff_chunk:(c + 1) * ff_chunk],
                        preferred_element_type=F32)
            a = jnp.maximum(a, 0.0)
            ff = ff + jnp.dot((a * a).astype(BF16), w2_ref[c * ff_chunk:(c + 1) * ff_chunk, :],
                              preferred_element_type=F32)
        out_ref[rows, :] = _rmsnorm(x1 + ff, gfin_ref[...])

    for i in range(x_ref.shape[0] // sub):
        rows_of(slice(i * sub, (i + 1) * sub))


def _out_ffn(x, o_list, lse_list, y_pool, y_mem, w_out, g_ffn, w_ff1, w_ff2, g_final,
             *, tm=1024, sub=512, ff_chunk=1024):
    B, S, _ = x.shape
    row = lambda b, t: (b, t, 0)
    tile = lambda width: pl.BlockSpec((None, tm, width), row)
    return pl.pallas_call(
        functools.partial(_out_ffn_kernel, sub=sub, ff_chunk=ff_chunk),
        grid=(B, S // tm),
        in_specs=[
            tile(D_MODEL),
            tile(D_ATTN), tile(D_ATTN),
            tile(N_ATTN_HEADS), tile(N_ATTN_HEADS),
            tile(D_POOL), tile(D_XMEM),
            _resident(w_out.shape),
            _resident((1, D_MODEL)),
            _resident(w_ff1.shape),
            _resident(w_ff2.shape),
            _resident((1, D_MODEL)),
        ],
        out_specs=tile(D_MODEL),
        out_shape=jax.ShapeDtypeStruct((B, S, D_MODEL), F32),
        compiler_params=pltpu.CompilerParams(
            dimension_semantics=("parallel", "parallel"),
            vmem_limit_bytes=VMEM_LIMIT_BYTES),
    )(x, *o_list, *lse_list, y_pool, y_mem, w_out, g_ffn, w_ff1, w_ff2, g_final)


def _block_diag(w_pool):
    G, C, E = w_pool.shape
    out = jnp.zeros((G * C, G * E), w_pool.dtype)
    for g in range(G):
        out = out.at[g * C:(g + 1) * C, g * E:(g + 1) * E].set(w_pool[g])
    return out


def kernel(x, mem, g_mix, w_in, g_mem, w_mem_kv, w_pool, pool_scale, w_out,
           g_ffn, w_ff1, w_ff2, g_final):
    depth = w_in.shape[0]
    for i in range(depth):
        q, k, v, qr, kr, vr, qm, y_pool, w_out16, w_ff1_16, w_ff2_16 = _inproj(
            x, g_mix[i][None], w_in[i].astype(BF16),
            _block_diag(w_pool[i]).astype(BF16), pool_scale[i][None],
            (w_out[i], w_ff1[i], w_ff2[i]))

        o1, lse1, y_mem = _band_attn(q, k, v, qm, mem, g_mem[i][None], w_mem_kv[i])
        oa, lsea = _dil_attn(qr, kr, vr)

        assert depth == 1
        x = _out_ffn(x, [o1, oa], [lse1, lsea], y_pool, y_mem, w_out16,
                     g_ffn[i][None], w_ff1_16, w_ff2_16, g_final[None])
    return x
```

```python
import functools

import numpy as np
import jax
import jax.numpy as jnp
from jax import lax
from jax.experimental import pallas as pl
from jax.experimental.pallas import tpu as pltpu

D_MODEL = 1024
HEAD_DIM = 64
N_ATTN_HEADS = 8
N_MEM_HEADS = 4
POOL_WINDOWS = (2, 4, 8, 16)
POOL_GROUP_DIM = 64
D_ATTN = N_ATTN_HEADS * HEAD_DIM
D_POOL = len(POOL_WINDOWS) * POOL_GROUP_DIM
D_XMEM = N_MEM_HEADS * HEAD_DIM
DILATED = ((128, 1), (512, 4), (2048, 16))
BLK = 128
SPAN = 128
D_FF = 4 * D_MODEL
EPS = 1e-6
POOL_HALO = 16

N_RES = 16
TILE = N_RES * BLK
PAIR = 2 * HEAD_DIM
LANES = 128
PERM_ROWS = 256
PERM_PER = PERM_ROWS // N_RES
N_PAIRS = N_ATTN_HEADS // 2
N_MEM_PAIRS = N_MEM_HEADS // 2

BF16 = jnp.bfloat16
F32 = jnp.float32
LOG2E = 1.4426950408889634
LN2 = 0.6931471805599453

VMEM_LIMIT_BYTES = 56 * 1024 * 1024

assert all(w // d == SPAN for w, d in DILATED)
assert [d for _, d in DILATED] == [1, 4, N_RES]
assert max(POOL_WINDOWS) <= POOL_HALO
assert all(w & (w - 1) == 0 for w in POOL_WINDOWS) and list(POOL_WINDOWS) == sorted(POOL_WINDOWS)


def _rmsnorm(x, g):
    ms = jnp.mean(x * x, axis=-1, keepdims=True)
    return (x * lax.rsqrt(ms + EPS)) * g


def _resident(shape):
    return pl.BlockSpec(shape, lambda *_: (0,) * len(shape), pipeline_mode=pl.Buffered(1))


def _residue_permutation(rows):
    per = rows // N_RES
    p = np.zeros((rows, rows), np.float32)
    for r in range(N_RES):
        for i in range(per):
            p[r * per + i, N_RES * i + r] = 1.0
    return p


def _inproj_kernel(x_ref, g_ref, w_ref, wpool_ref, pscale_ref, perm_ref, la_ref, lb_ref, lc_ref,
                   q_ref, k_ref, v_ref, qr_ref, kr_ref, vr_ref, qm_ref, yp_ref,
                   la16_ref, lb16_ref, lc16_ref, carry_ref, h_ref, w16_ref, *, tm, sub):
    t = pl.program_id(1)
    per = sub // N_RES

    @pl.when(jnp.logical_and(pl.program_id(0) == 0, t == 0))
    def _():
        w16_ref[...] = w_ref[...].astype(BF16)

    for src_ref, dst_ref in ((la_ref, la16_ref), (lb_ref, lb16_ref), (lc_ref, lc16_ref)):
        dst_ref[...] = src_ref[...].astype(BF16)

    @pl.when(t == 0)
    def _():
        carry_ref[...] = jnp.zeros((POOL_HALO, D_POOL), F32)

    def rows_of(i):
        rows = slice(i * sub, (i + 1) * sub)
        h_ref[rows, :] = _rmsnorm(x_ref[rows, :], g_ref[...]).astype(BF16)

        def proj(lo, hi):
            return jnp.dot(h_ref[rows, :], w16_ref[:, lo:hi], preferred_element_type=F32)

        u = proj(3 * D_ATTN, 3 * D_ATTN + D_POOL)
        run = jnp.concatenate([carry_ref[...], u], axis=0)
        carry_ref[...] = u[sub - POOL_HALO:, :]
        group = lax.broadcasted_iota(jnp.int32, (1, D_POOL), 1) // POOL_GROUP_DIM
        total = None
        width = 1
        for g, w in enumerate(POOL_WINDOWS):
            while width < w:
                run = run + pltpu.roll(run, width, 0)
                width *= 2
            total = run if total is None else jnp.where(group >= g, run, total)
        total = total[POOL_HALO:, :]
        win = jnp.zeros((1, D_POOL), jnp.int32)
        for g, w in enumerate(POOL_WINDOWS):
            win = jnp.where(group == g, w, win)
        pos = t * tm + i * sub + lax.broadcasted_iota(jnp.int32, (sub, 1), 0)
        cnt = jnp.minimum(win, pos + 1).astype(F32)
        d = (total / cnt - u).astype(BF16)

        scale = LOG2E * HEAD_DIM ** -0.5
        q_ref[rows, :] = (proj(0, D_ATTN) * scale).astype(BF16)
        k_ref[rows, :] = proj(D_ATTN, 2 * D_ATTN).astype(BF16)
        v_ref[rows, :] = proj(2 * D_ATTN, 3 * D_ATTN).astype(BF16)
        qm_ref[rows, :] = (proj(3 * D_ATTN + D_POOL, 3 * D_ATTN + D_POOL + D_XMEM)
                           * scale).astype(BF16)

        yp = jnp.dot(d, wpool_ref[...], preferred_element_type=F32) * pscale_ref[...]
        yp_ref[rows, :] = yp.astype(BF16)

        perm = perm_ref[...]
        for nat_ref, res_ref in ((q_ref, qr_ref), (k_ref, kr_ref), (v_ref, vr_ref)):
            for piece in range(sub // PERM_ROWS):
                src = slice(i * sub + piece * PERM_ROWS, i * sub + (piece + 1) * PERM_ROWS)
                dst = slice(i * per + piece * PERM_PER, i * per + (piece + 1) * PERM_PER)
                moved = jnp.dot(perm, nat_ref[src, :], preferred_element_type=F32).astype(BF16)
                res_ref[:, dst, :] = moved.reshape(N_RES, PERM_PER, D_ATTN)

    for i in range(tm // sub):
        rows_of(i)


def _inproj(x, g_mix, w_in, wpool_bd, pool_scale, later_weights, *, tm=1024, sub=512):
    B, S, _ = x.shape
    d_in = w_in.shape[1]
    per_tile = TILE // tm
    steps = B * (S // tm)
    chunk = lambda b, t: (b * (S // tm) + t, 0)
    later_specs = [pl.BlockSpec((w.shape[0] // steps, w.shape[1]), chunk) for w in later_weights]
    assert all(w.shape[0] % (steps * 16) == 0 for w in later_weights)
    perm = jnp.asarray(_residue_permutation(PERM_ROWS), dtype=BF16)
    row = lambda b, t: (b, t, 0)
    res = lambda b, t: (b, t // per_tile, 0, t % per_tile, 0)
    nat_spec = pl.BlockSpec((None, tm, D_ATTN), row)
    res_spec = pl.BlockSpec((None, None, N_RES, tm // N_RES, D_ATTN), res)
    nat_shape = jax.ShapeDtypeStruct((B, S, D_ATTN), BF16)
    res_shape = jax.ShapeDtypeStruct((B, S // TILE, N_RES, BLK, D_ATTN), BF16)
    return pl.pallas_call(
        functools.partial(_inproj_kernel, tm=tm, sub=sub),
        grid=(B, S // tm),
        in_specs=[
            pl.BlockSpec((None, tm, D_MODEL), row),
            _resident((1, D_MODEL)),
            _resident((D_MODEL, d_in)),
            _resident((D_POOL, D_POOL)),
            _resident((1, D_POOL)),
            _resident((PERM_ROWS, PERM_ROWS)),
            *later_specs,
        ],
        out_specs=[
            nat_spec, nat_spec, nat_spec, res_spec, res_spec, res_spec,
            pl.BlockSpec((None, tm, D_XMEM), row),
            pl.BlockSpec((None, tm, D_POOL), row),
            *later_specs,
        ],
        out_shape=[
            nat_shape, nat_shape, nat_shape, res_shape, res_shape, res_shape,
            jax.ShapeDtypeStruct((B, S, D_XMEM), BF16),
            jax.ShapeDtypeStruct((B, S, D_POOL), BF16),
            *[jax.ShapeDtypeStruct(w.shape, BF16) for w in later_weights],
        ],
        scratch_shapes=[pltpu.VMEM((POOL_HALO, D_POOL), F32),
                        pltpu.VMEM((tm, D_MODEL), BF16),
                        pltpu.VMEM((D_MODEL, d_in), BF16)],
        compiler_params=pltpu.CompilerParams(
            dimension_semantics=("arbitrary", "arbitrary"),
            vmem_limit_bytes=VMEM_LIMIT_BYTES),
    )(x, g_mix, w_in, wpool_bd, pool_scale, perm, *later_weights)


def _alibi_bias_table(dilation, q_off, k_off):
    slopes = (2.0 ** (-8.0 * np.arange(1, N_ATTN_HEADS + 1, dtype=np.float32)
                      / N_ATTN_HEADS)).astype(np.float32)
    rel = q_off[:, None] - k_off[None, :]
    valid = (rel >= 0) & (rel <= SPAN)
    bias = -slopes[:, None, None] * (dilation * rel).astype(np.float32)[None] * np.float32(LOG2E)
    full = np.where(valid[None], bias, -np.inf)
    first = np.where((valid & (k_off[None, :] >= BLK))[None], bias, -np.inf)
    table = np.stack([first, full]).astype(np.float32)
    return table.reshape(2, N_PAIRS, 2 * BLK, 2 * BLK)


def _pair_scores(qp, kp):
    is_lo = lax.broadcasted_iota(jnp.int32, (1, PAIR), 1) < HEAD_DIM
    zero = jnp.zeros_like(qp)
    q2 = jnp.concatenate([jnp.where(is_lo, qp, zero), jnp.where(is_lo, zero, qp)], axis=0)
    return lax.dot_general(q2, kp, (((1,), (1,)), ((), ())), preferred_element_type=F32)


def _pair_pv(p2, vp):
    vaug = jnp.concatenate([vp, jnp.ones_like(vp)], axis=1)
    return jnp.dot(p2.astype(BF16), vaug, preferred_element_type=F32)


def _pair_select(out2):
    R = out2.shape[0] // 2
    is_lo = lax.broadcasted_iota(jnp.int32, (1, PAIR), 1) < HEAD_DIM
    num = jnp.where(is_lo, out2[:R, :PAIR], out2[R:, :PAIR])
    den = jnp.where(is_lo, out2[:R, PAIR:], out2[R:, PAIR:])
    return num, den


def _band_attn_kernel(q_ref, kh_ref, kc_ref, vh_ref, vc_ref, bias_ref, qm_ref,
                      mem_ref, gmem_ref, wmem_ref,
                      o_ref, lse_ref, ym_ref,
                      sa_ref, sb_ref, ma_ref, mb_ref, pa_ref, pb_ref, sta_ref, stb_ref,
                      km_ref, vm_ref, *, tq):
    first_tile = pl.program_id(1) == 0
    nsub = tq // BLK

    @pl.when(first_tile)
    def _():
        mn = _rmsnorm(mem_ref[...], gmem_ref[...]).astype(BF16)
        kv = jnp.dot(mn, wmem_ref[...].astype(BF16), preferred_element_type=F32)
        km_ref[...] = kv[:, :D_XMEM].astype(BF16)
        vm_ref[...] = kv[:, D_XMEM:].astype(BF16)

    def window(halo_ref, cur_ref, j, cols):
        if j == 0:
            return jnp.concatenate([halo_ref[:, cols], cur_ref[0:BLK, cols]], axis=0)
        return cur_ref[(j - 1) * BLK:(j + 1) * BLK, cols]

    def scores(j, bufs):
        s_ref, m_ref, _, _ = bufs
        r0 = j * BLK
        sel = jnp.where(first_tile, 0, 1) if j == 0 else 1
        for hp in range(N_PAIRS):
            cols = slice(hp * PAIR, (hp + 1) * PAIR)
            s2 = _pair_scores(q_ref[pl.ds(r0, BLK), cols], window(kh_ref, kc_ref, j, cols))
            s2 = s2 + bias_ref[sel, hp]
            s_ref[hp] = s2
            m_ref[hp] = jnp.broadcast_to(jnp.max(s2, axis=-1, keepdims=True), (2 * BLK, BLK))
        for mp in range(N_MEM_PAIRS):
            cols = slice(mp * PAIR, (mp + 1) * PAIR)
            s2 = _pair_scores(qm_ref[pl.ds(r0, BLK), cols], km_ref[:, cols])
            s_ref[N_PAIRS + mp] = s2
            m_ref[N_PAIRS + mp] = jnp.broadcast_to(
                jnp.max(s2, axis=-1, keepdims=True), (2 * BLK, BLK))

    def probs(j, bufs):
        s_ref, m_ref, p_ref, st_ref = bufs
        for u in range(N_PAIRS + N_MEM_PAIRS):
            top = m_ref[u]
            p_ref[u] = jnp.exp2(s_ref[u] - jnp.concatenate([top, top], axis=1)).astype(BF16)
        for h in range(N_ATTN_HEADS):
            rows = slice((h % 2) * BLK, (h % 2 + 1) * BLK)
            st_ref[0, :, h:h + 1] = m_ref[h // 2, rows, h:h + 1]

    def finish(j, bufs):
        _, _, p_ref, st_ref = bufs
        r0 = j * BLK
        for hp in range(N_PAIRS):
            cols = slice(hp * PAIR, (hp + 1) * PAIR)
            out2 = _pair_pv(p_ref[hp], window(vh_ref, vc_ref, j, cols))
            num, den = _pair_select(out2)
            o_ref[pl.ds(r0, BLK), cols] = (num / den).astype(BF16)
            for hh in range(2):
                rows = slice(hh * BLK, (hh + 1) * BLK)
                h = 2 * hp + hh
                st_ref[1, :, h:h + 1] = out2[rows, PAIR + h:PAIR + h + 1]
        lse_ref[pl.ds(r0, BLK), :] = LN2 * (st_ref[0, :, 0:N_ATTN_HEADS]
                                            + jnp.log2(st_ref[1, :, 0:N_ATTN_HEADS]))
        for mp in range(N_MEM_PAIRS):
            cols = slice(mp * PAIR, (mp + 1) * PAIR)
            num, den = _pair_select(_pair_pv(p_ref[N_PAIRS + mp], vm_ref[:, cols]))
            ym_ref[pl.ds(r0, BLK), cols] = (num / den).astype(BF16)

    assert nsub % 2 == 0 and nsub >= 4
    even = (sa_ref, ma_ref, pa_ref, sta_ref)
    odd = (sb_ref, mb_ref, pb_ref, stb_ref)
    scores(0, even)
    scores(1, odd)
    probs(0, even)

    for n in range(1, nsub - 1, 2):
        scores(n + 1, even)
        probs(n, odd)
        finish(n - 1, even)
        scores(n + 2, odd)
        probs(n + 1, even)
        finish(n, odd)
    probs(nsub - 1, odd)
    finish(nsub - 2, even)
    finish(nsub - 1, odd)


def _band_attn(q, k, v, qm, mem, g_mem, w_mem_kv, *, tq=4096):
    N, L, _ = q.shape
    M = mem.shape[1]
    assert M == 2 * BLK
    sub = tq // BLK
    offs = np.arange(2 * BLK)
    bias = jnp.asarray(_alibi_bias_table(1, BLK + offs[:BLK], offs))
    cur = lambda n, i: (n, i, 0)
    halo = lambda n, i: (n, jnp.maximum(i * sub - 1, 0), 0)
    return pl.pallas_call(
        functools.partial(_band_attn_kernel, tq=tq),
        grid=(N, L // tq),
        in_specs=[
            pl.BlockSpec((None, tq, D_ATTN), cur),
            pl.BlockSpec((None, BLK, D_ATTN), halo),
            pl.BlockSpec((None, tq, D_ATTN), cur),
            pl.BlockSpec((None, BLK, D_ATTN), halo),
            pl.BlockSpec((None, tq, D_ATTN), cur),
            _resident(bias.shape),
            pl.BlockSpec((None, tq, D_XMEM), cur),
            pl.BlockSpec((None, M, D_MODEL), lambda n, i: (n, 0, 0)),
            _resident((1, D_MODEL)),
            _resident((D_MODEL, 2 * D_XMEM)),
        ],
        out_specs=[
            pl.BlockSpec((None, tq, D_ATTN), cur),
            pl.BlockSpec((None, tq, N_ATTN_HEADS), cur),
            pl.BlockSpec((None, tq, D_XMEM), cur),
        ],
        out_shape=[
            jax.ShapeDtypeStruct((N, L, D_ATTN), BF16),
            jax.ShapeDtypeStruct((N, L, N_ATTN_HEADS), F32),
            jax.ShapeDtypeStruct((N, L, D_XMEM), BF16),
        ],
        scratch_shapes=[pltpu.VMEM((N_PAIRS + N_MEM_PAIRS, 2 * BLK, 2 * BLK), F32)] * 2
        + [pltpu.VMEM((N_PAIRS + N_MEM_PAIRS, 2 * BLK, BLK), F32)] * 2
        + [pltpu.VMEM((N_PAIRS + N_MEM_PAIRS, 2 * BLK, 2 * BLK), BF16)] * 2
        + [pltpu.VMEM((2, BLK, LANES), F32)] * 2
        + [pltpu.VMEM((M, D_XMEM), BF16)] * 2,
        compiler_params=pltpu.CompilerParams(
            dimension_semantics=("arbitrary", "arbitrary"),
            vmem_limit_bytes=VMEM_LIMIT_BYTES),
    )(q, k, k, v, v, bias, qm, mem, g_mem, w_mem_kv)


Q4 = BLK // 4


def _dil_attn_kernel(q_ref, kp_ref, kc_ref, vp_ref, vc_ref, b16_ref, b4_ref, unperm_ref,
                     o_ref, lse_ref,
                     sa_ref, sb_ref, ma_ref, mb_ref, pa_ref, pb_ref, ta_ref, tb_ref,
                     n16_ref, d16_ref, m16_ref, op_ref, lp_ref):
    sel_first = jnp.where(pl.program_id(1) == 0, 0, 1)
    is_lo = lax.broadcasted_iota(jnp.int32, (1, PAIR), 1) < HEAD_DIM
    lane = lax.broadcasted_iota(jnp.int32, (1, PAIR), 1)
    cols = lambda hp: slice(hp * PAIR, (hp + 1) * PAIR)

    def scores_into(bufs, q_of, k_of, bias_of):
        s_ref, m_ref, _, _ = bufs
        for hp in range(N_PAIRS):
            s2 = _pair_scores(q_of(hp), k_of(hp)) + bias_of(hp)
            s_ref[hp] = s2
            m_ref[hp] = jnp.broadcast_to(jnp.max(s2, axis=-1, keepdims=True), (2 * BLK, BLK))

    def probs(bufs):
        s_ref, m_ref, p_ref, top_ref = bufs
        for hp in range(N_PAIRS):
            top = m_ref[hp]
            p_ref[hp] = jnp.exp2(s_ref[hp] - jnp.concatenate([top, top], axis=1)).astype(BF16)
            top_ref[hp] = jnp.where(is_lo, top[:BLK], top[BLK:])

    def pv_from(bufs, v_of, hp):
        _, _, p_ref, top_ref = bufs
        num, den = _pair_select(_pair_pv(p_ref[hp], v_of(hp)))
        return num, den, top_ref[hp]

    def group_blocks(r4, keep):
        slabs = [r4 + 4 * a for a in range(4)]

        def window16(prev_ref, cur_ref, a):
            return lambda hp: jnp.concatenate(
                [prev_ref[slabs[a], :, cols(hp)], cur_ref[slabs[a], :, cols(hp)]], axis=0)

        def window4(prev_ref, cur_ref, c):
            def rows(hp):
                parts = []
                for s in slabs:
                    if c == 0:
                        parts += [prev_ref[s, BLK - Q4:BLK, cols(hp)], cur_ref[s, 0:Q4, cols(hp)]]
                    else:
                        parts.append(cur_ref[s, (c - 1) * Q4:(c + 1) * Q4, cols(hp)])
                return jnp.concatenate(parts, axis=0)
            return rows

        def quarter_rows(ref, c):
            return lambda hp: jnp.concatenate(
                [ref[s, c * Q4:(c + 1) * Q4, cols(hp)] for s in slabs], axis=0)

        def score16(a, bufs):
            scores_into(bufs, lambda hp: q_ref[slabs[a], :, cols(hp)],
                        window16(kp_ref, kc_ref, a), lambda hp: b16_ref[sel_first, hp])

        def finish16(a, bufs):
            for hp in range(N_PAIRS):
                num, den, top = pv_from(bufs, window16(vp_ref, vc_ref, a), hp)
                n16_ref[keep + a, :, cols(hp)] = num
                d16_ref[keep + a, :, cols(hp)] = den
                m16_ref[keep + a, :, cols(hp)] = top

        def score4(c, bufs):
            sel = sel_first if c == 0 else 1
            scores_into(bufs, quarter_rows(q_ref, c), window4(kp_ref, kc_ref, c),
                        lambda hp: b4_ref[sel, hp])

        def finish4(c, bufs):
            gather = lambda ref, hp: jnp.concatenate(
                [ref[keep + a, c * Q4:(c + 1) * Q4, cols(hp)] for a in range(4)], axis=0)
            lse_tile = jnp.zeros((BLK, PAIR), F32)
            for hp in range(N_PAIRS):
                num4, den4, top4 = pv_from(bufs, window4(vp_ref, vc_ref, c), hp)
                top16 = gather(m16_ref, hp)
                top = jnp.maximum(top16, top4)
                w16 = jnp.exp2(top16 - top)
                w4 = jnp.exp2(top4 - top)
                num = w16 * gather(n16_ref, hp) + w4 * num4
                den = w16 * gather(d16_ref, hp) + w4 * den4
                o = (num / den).astype(BF16)
                lse = LN2 * (top + jnp.log2(den))
                for a, s in enumerate(slabs):
                    op_ref[s, c * Q4:(c + 1) * Q4, cols(hp)] = o[a * Q4:(a + 1) * Q4]
                at = N_ATTN_HEADS * c + 2 * hp
                lse_tile = jnp.where(lane == at, lse, lse_tile)
                lse_tile = jnp.where(lane == at + 1, pltpu.roll(lse, HEAD_DIM, 1), lse_tile)
            for a, s in enumerate(slabs):
                part = lse_tile[a * Q4:(a + 1) * Q4]
                lp_ref[s] = part if c == 0 else lp_ref[s] + part

        return ([(functools.partial(score16, a), functools.partial(finish16, a))
                 for a in range(4)]
                + [(functools.partial(score4, c), functools.partial(finish4, c))
                   for c in range(4)])

    bufs = ((sa_ref, ma_ref, pa_ref, ta_ref), (sb_ref, mb_ref, pb_ref, tb_ref))

    def group(g, carry):
        blocks = group_blocks(2 * g, 0) + group_blocks(2 * g + 1, 4)
        blocks[0][0](bufs[0])
        for n in range(len(blocks) + 1):
            if n + 1 < len(blocks):
                blocks[n + 1][0](bufs[(n + 1) % 2])
            if n < len(blocks):
                probs(bufs[n % 2])
            if n >= 1:
                blocks[n - 1][1](bufs[(n - 1) % 2])
        return carry

    lax.fori_loop(0, 2, group, 0)

    unperm = unperm_ref[...]
    for n in range(BLK // PERM_PER):
        rows = slice(n * PERM_PER, (n + 1) * PERM_PER)
        o_res = jnp.concatenate([op_ref[r, rows, :] for r in range(N_RES)], axis=0)
        o_ref[n * PERM_ROWS:(n + 1) * PERM_ROWS, :] = jnp.dot(
            unperm, o_res, preferred_element_type=F32).astype(BF16)
    for part in range(Q4 // PERM_PER):
        rows = slice(part * PERM_PER, (part + 1) * PERM_PER)
        l_res = jnp.concatenate([lp_ref[r, rows, :] for r in range(N_RES)], axis=0)
        hi = l_res.astype(BF16)
        rest = l_res - hi.astype(F32)
        mid = rest.astype(BF16)
        low = (rest - mid.astype(F32)).astype(BF16)
        l_nat = (jnp.dot(unperm, hi, preferred_element_type=F32)
                 + jnp.dot(unperm, mid, preferred_element_type=F32)
                 + jnp.dot(unperm, low, preferred_element_type=F32))
        for c in range(BLK // Q4):
            first = (c * Q4 + part * PERM_PER) * N_RES
            lse_ref[first:first + PERM_ROWS, :] = l_nat[:, N_ATTN_HEADS * c:N_ATTN_HEADS * (c + 1)]


def _dil_attn(qr, kr, vr):
    B, T = qr.shape[:2]
    S = T * TILE
    offs = np.arange(2 * BLK)
    b16 = jnp.asarray(_alibi_bias_table(16, BLK + offs[:BLK], offs))
    qo = np.arange(BLK)
    b4 = jnp.asarray(_alibi_bias_table(
        4, BLK + 4 * (qo % Q4) + qo // Q4, 4 * (offs % (2 * Q4)) + offs // (2 * Q4)))
    unperm = jnp.asarray(_residue_permutation(PERM_ROWS).T, dtype=BF16)
    slab_block = (None, None, N_RES, BLK, D_ATTN)
    cur = lambda b, t: (b, t, 0, 0, 0)
    prev = lambda b, t: (b, jnp.maximum(t - 1, 0), 0, 0, 0)
    nat = lambda b, t: (b, t, 0)
    return pl.pallas_call(
        _dil_attn_kernel,
        grid=(B, T),
        in_specs=[
            pl.BlockSpec(slab_block, cur),
            pl.BlockSpec(slab_block, prev),
            pl.BlockSpec(slab_block, cur),
            pl.BlockSpec(slab_block, prev),
            pl.BlockSpec(slab_block, cur),
            _resident(b16.shape),
            _resident(b4.shape),
            _resident(unperm.shape),
        ],
        out_specs=[
            pl.BlockSpec((None, TILE, D_ATTN), nat),
            pl.BlockSpec((None, TILE, N_ATTN_HEADS), nat),
        ],
        out_shape=[
            jax.ShapeDtypeStruct((B, S, D_ATTN), BF16),
            jax.ShapeDtypeStruct((B, S, N_ATTN_HEADS), F32),
        ],
        scratch_shapes=[pltpu.VMEM((N_PAIRS, 2 * BLK, 2 * BLK), F32)] * 2
        + [pltpu.VMEM((N_PAIRS, 2 * BLK, BLK), F32)] * 2
        + [pltpu.VMEM((N_PAIRS, 2 * BLK, 2 * BLK), BF16)] * 2
        + [pltpu.VMEM((N_PAIRS, BLK, PAIR), F32)] * 2
        + [pltpu.VMEM((2 * 4, BLK, D_ATTN), F32)] * 3
        + [pltpu.VMEM((N_RES, BLK, D_ATTN), BF16), pltpu.VMEM((N_RES, Q4, PAIR), F32)],
        compiler_params=pltpu.CompilerParams(
            dimension_semantics=("parallel", "parallel"),
            vmem_limit_bytes=VMEM_LIMIT_BYTES),
    )(qr, kr, kr, vr, vr, b16, b4, unperm)


def _out_ffn_kernel(x_ref, o1_ref, oa_ref, l1_ref, la_ref,
                    yp_ref, ym_ref, wout_ref, gffn_ref, w1_ref, w2_ref,
                    gfin_ref, out_ref, *, sub, ff_chunk):
    def rows_of(rows):
        l1, la = l1_ref[rows, :], la_ref[rows, :]
        top = jnp.maximum(l1, la)
        e1, ea = jnp.exp(l1 - top), jnp.exp(la - top)
        w1 = e1 / (e1 + ea)
        is_lo = lax.broadcasted_iota(jnp.int32, (1, PAIR), 1) < HEAD_DIM
        w1 = jnp.concatenate(
            [jnp.where(is_lo, w1[:, 2 * hp:2 * hp + 1], w1[:, 2 * hp + 1:2 * hp + 2])
             for hp in range(N_PAIRS)], axis=1)
        oa = oa_ref[rows, :].astype(F32)
        y_attn = oa + w1 * (o1_ref[rows, :].astype(F32) - oa)

        mix = jnp.dot(y_attn.astype(BF16), wout_ref[0:D_ATTN, :], preferred_element_type=F32)
        mix = mix + jnp.dot(yp_ref[rows, :], wout_ref[D_ATTN:D_ATTN + D_POOL, :],
                            preferred_element_type=F32)
        mix = mix + jnp.dot(ym_ref[rows, :], wout_ref[D_ATTN + D_POOL:, :],
                            preferred_element_type=F32)
        x1 = x_ref[rows, :] + mix

        h2 = _rmsnorm(x1, gffn_ref[...]).astype(BF16)
        ff = jnp.zeros(x1.shape, F32)
        for c in range(D_FF // ff_chunk):
            a = jnp.dot(h2, w1_ref[:, c * ff_chunk:(c + 1) * ff_chunk],
                        preferred_element_type=F32)
            a = jnp.maximum(a, 0.0)
            ff = ff + jnp.dot((a * a).astype(BF16), w2_ref[c * ff_chunk:(c + 1) * ff_chunk, :],
                              preferred_element_type=F32)
        out_ref[rows, :] = _rmsnorm(x1 + ff, gfin_ref[...])

    for i in range(x_ref.shape[0] // sub):
        rows_of(slice(i * sub, (i + 1) * sub))


def _out_ffn(x, o_list, lse_list, y_pool, y_mem, w_out, g_ffn, w_ff1, w_ff2, g_final,
             *, tm=1024, sub=512, ff_chunk=1024):
    B, S, _ = x.shape
    row = lambda b, t: (b, t, 0)
    tile = lambda width: pl.BlockSpec((None, tm, width), row)
    return pl.pallas_call(
        functools.partial(_out_ffn_kernel, sub=sub, ff_chunk=ff_chunk),
        grid=(B, S // tm),
        in_specs=[
            tile(D_MODEL),
            tile(D_ATTN), tile(D_ATTN),
            tile(N_ATTN_HEADS), tile(N_ATTN_HEADS),
            tile(D_POOL), tile(D_XMEM),
            _resident(w_out.shape),
            _resident((1, D_MODEL)),
            _resident(w_ff1.shape),
            _resident(w_ff2.shape),
            _resident((1, D_MODEL)),
        ],
        out_specs=tile(D_MODEL),
        out_shape=jax.ShapeDtypeStruct((B, S, D_MODEL), F32),
        compiler_params=pltpu.CompilerParams(
            dimension_semantics=("parallel", "parallel"),
            vmem_limit_bytes=VMEM_LIMIT_BYTES),
    )(x, *o_list, *lse_list, y_pool, y_mem, w_out, g_ffn, w_ff1, w_ff2, g_final)


def _block_diag(w_pool):
    G, C, E = w_pool.shape
    out = jnp.zeros((G * C, G * E), w_pool.dtype)
    for g in range(G):
        out = out.at[g * C:(g + 1) * C, g * E:(g + 1) * E].set(w_pool[g])
    return out


def kernel(x, mem, g_mix, w_in, g_mem, w_mem_kv, w_pool, pool_scale, w_out,
           g_ffn, w_ff1, w_ff2, g_final):
    depth = w_in.shape[0]
    for i in range(depth):
        q, k, v, qr, kr, vr, qm, y_pool, w_out16, w_ff1_16, w_ff2_16 = _inproj(
            x, g_mix[i][None], w_in[i],
            _block_diag(w_pool[i]).astype(BF16), pool_scale[i][None],
            (w_out[i], w_ff1[i], w_ff2[i]))

        o1, lse1, y_mem = _band_attn(q, k, v, qm, mem, g_mem[i][None], w_mem_kv[i])
        oa, lsea = _dil_attn(qr, kr, vr)

        assert depth == 1
        x = _out_ffn(x, [o1, oa], [lse1, lsea], y_pool, y_mem, w_out16,
                     g_ffn[i][None], w_ff1_16, w_ff2_16, g_final[None])
    return x
```
